```python
import math
import jax, jax.numpy as jnp
from jax import lax
import numpy as np

D_MODEL = 2048
BATCH = 8
SEQ = 2048
DEPTH = 2
DEC_BATCH = 128
DEC_SEQ = 1
PAST_LEN = 2048
PAGE_SIZE = 128

N_EVEN = (DEPTH + 1) // 2
N_ODD = DEPTH // 2
ATTN_WIDTH = D_MODEL // 2
HEAD_DIM = 128
N_HEADS = ATTN_WIDTH // HEAD_DIM
ATTN_SCALE = HEAD_DIM ** -0.5
MOBA_BLOCK = 256
MOBA_TOPK = 3
ROPE_THETA = 10000.0
Q_CHUNK = 32
S5_WIDTH = D_MODEL - ATTN_WIDTH
S5_GROUP = 16
S5_GROUPS = S5_WIDTH // S5_GROUP
S5_STATE = 64
S5_DT_MIN = 0.001
S5_DT_MAX = 0.1
EVEN_IN = 4 * ATTN_WIDTH + 2 * S5_WIDTH
SSD_WIDTH = 2 * D_MODEL
SSD_HEAD_DIM = 64
SSD_HEADS = SSD_WIDTH // SSD_HEAD_DIM
SSD_GROUPS = 8
SSD_STATE = 128
SSD_CONV = 4
SSD_CHUNK = 128
SSD_CONV_DIM = SSD_WIDTH + 2 * SSD_GROUPS * SSD_STATE
ODD_IN = SSD_WIDTH + SSD_CONV_DIM + SSD_HEADS
EPS = 1e-6

kernel_name = "moba_s5_mamba2_hybrid_step"


def rms_norm(x, w):
    xf = x.astype(jnp.float32)
    xf = xf * lax.rsqrt(jnp.mean(xf * xf, axis=-1, keepdims=True) + EPS)
    return (xf * w.astype(jnp.float32)).astype(x.dtype)


def rope(x, pos):
    half = HEAD_DIM // 2
    inv = ROPE_THETA ** (-jnp.arange(half, dtype=jnp.float32) / half)
    ang = pos.astype(jnp.float32)[:, None] * inv[None, :]
    cos = jnp.cos(ang)[:, None, :]
    sin = jnp.sin(ang)[:, None, :]
    xf = x.astype(jnp.float32)
    x1, x2 = xf[..., :half], xf[..., half:]
    return jnp.concatenate([x1 * cos - x2 * sin, x2 * cos + x1 * sin], axis=-1).astype(x.dtype)


def even_in_proj(h, pos, w_in, q_norm_w, k_norm_w):
    n, t, _ = h.shape
    proj = h @ w_in
    q, k, v, gate_a, u, gate_b = jnp.split(
        proj, [ATTN_WIDTH, 2 * ATTN_WIDTH, 3 * ATTN_WIDTH, 4 * ATTN_WIDTH, 4 * ATTN_WIDTH + S5_WIDTH], axis=-1)
    shp = (n, t, N_HEADS, HEAD_DIM)
    q = rope(rms_norm(q.reshape(shp), q_norm_w), pos)
    k = rope(rms_norm(k.reshape(shp), k_norm_w), pos)
    return q, k, v.reshape(shp), gate_a, u, gate_b


def moba_n_blocks(length):
    return max(-(-length // MOBA_BLOCK), MOBA_TOPK)


def to_blocks(k, n_blocks):
    length = k.shape[-3]
    pad = [(0, 0)] * k.ndim
    pad[-3] = (0, n_blocks * MOBA_BLOCK - length)
    k = jnp.pad(k, pad)
    return k.reshape(k.shape[:-3] + (n_blocks, MOBA_BLOCK) + k.shape[-2:])


def moba_query(q, q_pos, kb, vb, k_mean):
    f32 = jnp.float32
    nb = kb.shape[0]
    own = q_pos // MOBA_BLOCK
    s_blk = jnp.einsum('thd,nhd->thn', q.astype(f32), k_mean)
    fully_past = jnp.arange(nb, dtype=jnp.int32)[None, None, :] < own[:, None, None]
    s_blk = jnp.where(fully_past, s_blk, -jnp.inf)
    _, top_i = lax.top_k(s_blk, MOBA_TOPK)
    sel_ok = top_i < own[:, None, None]
    idx = jnp.concatenate([top_i, jnp.broadcast_to(own[:, None, None], top_i.shape[:2] + (1,))], axis=-1)
    ok = jnp.concatenate([sel_ok, jnp.ones(top_i.shape[:2] + (1,), dtype=bool)], axis=-1)
    heads = jnp.arange(N_HEADS)[None, :, None]
    kg = kb.transpose(2, 0, 1, 3)[heads, idx]
    vg = vb.transpose(2, 0, 1, 3)[heads, idx]
    key_pos = idx[..., None] * MOBA_BLOCK + jnp.arange(MOBA_BLOCK, dtype=jnp.int32)
    mask = ok[..., None] & (key_pos <= q_pos[:, None, None, None])
    s = jnp.einsum('thd,thkjd->thkj', q, kg, preferred_element_type=f32) * ATTN_SCALE
    p = jax.nn.softmax(jnp.where(mask, s, -jnp.inf), axis=(-2, -1))
    out = jnp.einsum('thkj,thkjd->thd', p, vg.astype(f32))
    return out.astype(q.dtype)


def moba_prompt(q, k, v):
    n, t = q.shape[:2]
    nb = moba_n_blocks(t)
    kb = to_blocks(k, nb)
    vb = to_blocks(v, nb)
    k_mean = jnp.mean(kb.astype(jnp.float32), axis=2)
    n_chunks = t // Q_CHUNK
    qc = q.reshape(n * n_chunks, Q_CHUNK, N_HEADS, HEAD_DIM)
    cid = jnp.arange(n * n_chunks, dtype=jnp.int32)

    def body(args):
        qi, ci = args
        b = ci // n_chunks
        pos = (ci % n_chunks) * Q_CHUNK + jnp.arange(Q_CHUNK, dtype=jnp.int32)
        return moba_query(qi, pos, kb[b], vb[b], k_mean[b])

    out = lax.map(body, (qc, cid))
    return out.reshape(n, t, ATTN_WIDTH)


def moba_sample(q, k, v, cache_k, cache_v, page_table):
    n, t = q.shape[:2]
    n_past = page_table.shape[1] * cache_k.shape[1]
    nb = moba_n_blocks(n_past + t)
    pos = n_past + jnp.arange(t, dtype=jnp.int32)

    def body(args):
        qi, ki, vi, pt = args
        kp = cache_k[pt].reshape(n_past, N_HEADS, HEAD_DIM)
        vp = cache_v[pt].reshape(n_past, N_HEADS, HEAD_DIM)
        kb = to_blocks(jnp.concatenate([kp, ki.astype(kp.dtype)], axis=0), nb)
        vb = to_blocks(jnp.concatenate([vp, vi.astype(vp.dtype)], axis=0), nb)
        k_mean = jnp.mean(kb.astype(jnp.float32), axis=1)
        return moba_query(qi, pos, kb, vb, k_mean)

    out = lax.map(body, (q, k, v, page_table))
    return out.reshape(n, t, ATTN_WIDTH)


def complex_affine(e1, e2):
    a1r, a1i, b1r, b1i = e1
    a2r, a2i, b2r, b2i = e2
    return (a2r * a1r - a2i * a1i, a2r * a1i + a2i * a1r,
            a2r * b1r - a2i * b1i + b2r, a2r * b1i + a2i * b1r + b2i)


def s5_branch(u, x0, lam_re, lam_im, log_dt, b_ri, c_ri, d_skip, glu_w, glu_b):
    f32 = jnp.float32
    n, t, _ = u.shape
    uf = u.astype(f32)
    ug = uf.reshape(n, t, S5_GROUPS, S5_GROUP)
    dt = jnp.exp(log_dt.astype(f32))[:, None]
    lr, li = lam_re.astype(f32), lam_im.astype(f32)
    mag = jnp.exp(lr * dt)
    ar, ai = mag * jnp.cos(li * dt), mag * jnp.sin(li * dt)
    d2 = lr * lr + li * li
    cr = ((ar - 1.0) * lr + ai * li) / d2
    ci = (ai * lr - (ar - 1.0) * li) / d2
    br, bi = b_ri[..., 0].astype(f32), b_ri[..., 1].astype(f32)
    bbr = cr[..., None] * br - ci[..., None] * bi
    bbi = cr[..., None] * bi + ci[..., None] * br
    bur = jnp.einsum('ntgc,gpc->ntgp', ug, bbr)
    bui = jnp.einsum('ntgc,gpc->ntgp', ug, bbi)
    a_r = jnp.broadcast_to(ar, bur.shape)
    a_i = jnp.broadcast_to(ai, bur.shape)
    acr, aci, bcr, bci = lax.associative_scan(complex_affine, (a_r, a_i, bur, bui), axis=1)
    x0r = x0[..., 0].astype(f32)[:, None]
    x0i = x0[..., 1].astype(f32)[:, None]
    xr = acr * x0r - aci * x0i + bcr
    xi = acr * x0i + aci * x0r + bci
    c_re, c_im = c_ri[..., 0].astype(f32), c_ri[..., 1].astype(f32)
    y = jnp.einsum('ntgp,gcp->ntgc', xr, c_re) - jnp.einsum('ntgp,gcp->ntgc', xi, c_im)
    y = y.reshape(n, t, S5_WIDTH) + d_skip.astype(f32) * uf
    z = jax.nn.gelu(y)
    out = z * jax.nn.sigmoid(z @ glu_w.astype(f32) + glu_b.astype(f32))
    new_state = jnp.stack([xr[:, -1], xi[:, -1]], axis=-1)
    return out.astype(u.dtype), new_state.astype(x0.dtype)


def segsum(a):
    t = a.shape[-1]
    x = jnp.broadcast_to(a[..., None], a.shape + (t,))
    x = jnp.where(jnp.tril(jnp.ones((t, t), dtype=bool), -1), x, 0.0)
    cs = jnp.cumsum(x, axis=-2)
    return jnp.where(jnp.tril(jnp.ones((t, t), dtype=bool)), cs, -jnp.inf)


def ssd_chunked(xdt, a_dt, bm, cm, h0):
    n, t, nh, hp = xdt.shape
    q = min(SSD_CHUNK, t)
    pad = (-t) % q
    if pad:
        xdt = jnp.pad(xdt, ((0, 0), (0, pad), (0, 0), (0, 0)))
        a_dt = jnp.pad(a_dt, ((0, 0), (0, pad), (0, 0)))
        bm = jnp.pad(bm, ((0, 0), (0, pad), (0, 0), (0, 0)))
        cm = jnp.pad(cm, ((0, 0), (0, pad), (0, 0), (0, 0)))
    nc = (t + pad) // q
    r = nh // SSD_GROUPS
    x = xdt.reshape(n, nc, q, SSD_GROUPS, r, hp)
    a = a_dt.reshape(n, nc, q, SSD_GROUPS, r).transpose(0, 3, 4, 1, 2)
    b_ = bm.reshape(n, nc, q, SSD_GROUPS, SSD_STATE)
    c_ = cm.reshape(n, nc, q, SSD_GROUPS, SSD_STATE)
    a_cs = jnp.cumsum(a, axis=-1)
    l_mat = jnp.exp(segsum(a))
    y_diag = jnp.einsum('bclgn,bcsgn,bgrcls,bcsgrp->bclgrp', c_, b_, l_mat, x)
    decay_states = jnp.exp(a_cs[..., -1:] - a_cs)
    states = jnp.einsum('bclgn,bgrcl,bclgrp->bcgrpn', b_, decay_states, x)
    states = jnp.concatenate([h0.reshape(n, 1, SSD_GROUPS, r, hp, SSD_STATE), states], axis=1)
    a_chunk = jnp.pad(a_cs[..., -1], ((0, 0), (0, 0), (0, 0), (1, 0)))
    decay_chunk = jnp.exp(segsum(a_chunk))
    new_states = jnp.einsum('bgrzc,bcgrpn->bzgrpn', decay_chunk, states)
    states, final = new_states[:, :-1], new_states[:, -1]
    y_off = jnp.einsum('bclgn,bcgrpn,bgrcl->bclgrp', c_, states, jnp.exp(a_cs))
    y = (y_diag + y_off).reshape(n, nc * q, nh, hp)[:, :t]
    return y, final.reshape(n, nh, hp, SSD_STATE)


def odd_mix(h, conv_buf, h0, w_in, conv_w, conv_b, dt_bias, a_log, d_skip, norm_w, w_out):
    f32 = jnp.float32
    n, t, _ = h.shape
    proj = h @ w_in
    z, xbc, dt_raw = jnp.split(proj, [SSD_WIDTH, SSD_WIDTH + SSD_CONV_DIM], axis=-1)
    xfull = jnp.concatenate([conv_buf.astype(xbc.dtype), xbc], axis=1)
    conv = lax.conv_general_dilated(
        xfull, conv_w[:, None, :].astype(xfull.dtype), window_strides=(1,), padding='VALID',
        dimension_numbers=('NWC', 'WIO', 'NWC'), feature_group_count=SSD_CONV_DIM) + conv_b
    new_buf = xfull[:, -(SSD_CONV - 1):]
    xbc = jax.nn.silu(conv.astype(f32))
    xs, bm, cm = jnp.split(xbc, [SSD_WIDTH, SSD_WIDTH + SSD_GROUPS * SSD_STATE], axis=-1)
    xs = xs.reshape(n, t, SSD_HEADS, SSD_HEAD_DIM)
    bm = bm.reshape(n, t, SSD_GROUPS, SSD_STATE)
    cm = cm.reshape(n, t, SSD_GROUPS, SSD_STATE)
    dt = jax.nn.softplus(dt_raw.astype(f32) + dt_bias.astype(f32))
    a = -jnp.exp(a_log.astype(f32))
    y, h_new = ssd_chunked(xs * dt[..., None], dt * a, bm, cm, h0.astype(f32))
    y = y + d_skip.astype(f32)[:, None] * xs
    y = y.reshape(n, t, SSD_WIDTH) * jax.nn.silu(z.astype(f32))
    yg = y.reshape(n, t, SSD_GROUPS, SSD_WIDTH // SSD_GROUPS)
    yg = yg * lax.rsqrt(jnp.mean(yg * yg, axis=-1, keepdims=True) + EPS)
    y = yg.reshape(n, t, SSD_WIDTH) * norm_w.astype(f32)
    out = y.astype(h.dtype) @ w_out
    return out, new_buf, h_new.astype(h0.dtype)


def setup_inputs(seed: int = 0) -> dict:
    key = jax.random.key(seed)
    ks = jax.random.split(key, 32)
    f32 = jnp.float32
    n_pages = PAST_LEN // PAGE_SIZE
    n_used = DEC_BATCH * n_pages
    n_pool = n_used + n_used // 4

    def nrm(k, shape, scale):
        return jax.random.normal(k, shape, f32) * scale

    x_prompt = nrm(ks[0], (BATCH, SEQ, D_MODEL), 1.0)
    x_sample = nrm(ks[1], (DEC_BATCH, DEC_SEQ, D_MODEL), 1.0)
    cache_k = nrm(ks[2], (N_EVEN, n_pool, PAGE_SIZE, N_HEADS, HEAD_DIM), 1.0)
    cache_v = nrm(ks[3], (N_EVEN, n_pool, PAGE_SIZE, N_HEADS, HEAD_DIM), 1.0)
    page_table = jax.random.permutation(ks[4], n_pool)[:n_used].reshape(DEC_BATCH, n_pages).astype(jnp.int32)
    state_s5 = nrm(ks[5], (N_EVEN, DEC_BATCH, S5_GROUPS, S5_STATE, 2), 0.1)
    state_conv = nrm(ks[6], (N_ODD, DEC_BATCH, SSD_CONV - 1, SSD_CONV_DIM), 1.0)
    state_ssd = nrm(ks[7], (N_ODD, DEC_BATCH, SSD_HEADS, SSD_HEAD_DIM, SSD_STATE), 0.1)
    norm_w = 1.0 + nrm(ks[8], (DEPTH, D_MODEL), 0.05)
    w_in_even = nrm(ks[9], (N_EVEN, D_MODEL, EVEN_IN), D_MODEL ** -0.5)
    q_norm_w = 1.0 + nrm(ks[10], (N_EVEN, HEAD_DIM), 0.05)
    k_norm_w = 1.0 + nrm(ks[11], (N_EVEN, HEAD_DIM), 0.05)
    s5_lambda_re = -0.5 + nrm(ks[12], (N_EVEN, S5_GROUPS, S5_STATE), 0.01)
    s5_lambda_im = jnp.pi * jnp.arange(S5_STATE, dtype=f32) + nrm(ks[13], (N_EVEN, S5_GROUPS, S5_STATE), 0.01)
    s5_log_dt = jax.random.uniform(ks[14], (N_EVEN, S5_GROUPS), f32, math.log(S5_DT_MIN), math.log(S5_DT_MAX))
    s5_b = nrm(ks[15], (N_EVEN, S5_GROUPS, S5_STATE, S5_GROUP, 2), S5_GROUP ** -0.5)
    s5_c = nrm(ks[16], (N_EVEN, S5_GROUPS, S5_GROUP, S5_STATE, 2), 0.5)
    s5_d = nrm(ks[17], (N_EVEN, S5_WIDTH), 1.0)
    s5_glu_w = nrm(ks[18], (N_EVEN, S5_WIDTH, S5_WIDTH), S5_WIDTH ** -0.5)
    s5_glu_b = nrm(ks[19], (N_EVEN, S5_WIDTH), 0.02)
    w_out_even = nrm(ks[20], (N_EVEN, ATTN_WIDTH + S5_WIDTH, D_MODEL), (ATTN_WIDTH + S5_WIDTH) ** -0.5)
    w_in_odd = nrm(ks[21], (N_ODD, D_MODEL, ODD_IN), D_MODEL ** -0.5)
    conv_w = nrm(ks[22], (N_ODD, SSD_CONV, SSD_CONV_DIM), SSD_CONV ** -0.5)
    conv_b = nrm(ks[23], (N_ODD, SSD_CONV_DIM), 0.02)
    dt0 = jnp.exp(jax.random.uniform(ks[24], (N_ODD, SSD_HEADS), f32, math.log(0.001), math.log(0.1)))
    ssd_dt_bias = dt0 + jnp.log(-jnp.expm1(-dt0))
    ssd_a_log = jnp.log(jax.random.uniform(ks[25], (N_ODD, SSD_HEADS), f32, 1.0, 16.0))
    ssd_d = 1.0 + nrm(ks[26], (N_ODD, SSD_HEADS), 0.1)
    ssd_norm_w = 1.0 + nrm(ks[27], (N_ODD, SSD_WIDTH), 0.05)
    w_out_odd = nrm(ks[28], (N_ODD, SSD_WIDTH, D_MODEL), SSD_WIDTH ** -0.5)
    return {"x_prompt": x_prompt, "x_sample": x_sample, "cache_k": cache_k, "cache_v": cache_v,
            "page_table": page_table, "state_s5": state_s5, "state_conv": state_conv, "state_ssd": state_ssd,
            "norm_w": norm_w, "w_in_even": w_in_even, "q_norm_w": q_norm_w, "k_norm_w": k_norm_w,
            "s5_lambda_re": s5_lambda_re, "s5_lambda_im": s5_lambda_im, "s5_log_dt": s5_log_dt,
            "s5_b": s5_b, "s5_c": s5_c, "s5_d": s5_d, "s5_glu_w": s5_glu_w, "s5_glu_b": s5_glu_b,
            "w_out_even": w_out_even, "w_in_odd": w_in_odd, "conv_w": conv_w, "conv_b": conv_b,
            "ssd_dt_bias": ssd_dt_bias, "ssd_a_log": ssd_a_log, "ssd_d": ssd_d, "ssd_norm_w": ssd_norm_w,
            "w_out_odd": w_out_odd}


def reference(x_prompt, x_sample, cache_k, cache_v, page_table, state_s5, state_conv, state_ssd,
              norm_w, w_in_even, q_norm_w, k_norm_w, s5_lambda_re, s5_lambda_im, s5_log_dt,
              s5_b, s5_c, s5_d, s5_glu_w, s5_glu_b, w_out_even, w_in_odd, conv_w, conv_b,
              ssd_dt_bias, ssd_a_log, ssd_d, ssd_norm_w, w_out_odd):
    n_past = page_table.shape[1] * cache_k.shape[2]
    pos_p = jnp.arange(x_prompt.shape[1], dtype=jnp.int32)
    pos_s = n_past + jnp.arange(x_sample.shape[1], dtype=jnp.int32)
    yp, ys = x_prompt, x_sample
    k_p, v_p, k_s, v_s, s5_p, s5_s = [], [], [], [], [], []
    cv_p, cv_s, ssd_p, ssd_s = [], [], [], []
    for i in range(DEPTH):
        hp = rms_norm(yp, norm_w[i])
        hs = rms_norm(ys, norm_w[i])
        if i % 2 == 0:
            e = i // 2
            qp, kp, vp, gap, up, gbp = even_in_proj(hp, pos_p, w_in_even[e], q_norm_w[e], k_norm_w[e])
            qs, kss, vss, gas, us, gbs = even_in_proj(hs, pos_s, w_in_even[e], q_norm_w[e], k_norm_w[e])
            att_p = moba_prompt(qp, kp, vp)
            att_s = moba_sample(qs, kss, vss, cache_k[e], cache_v[e], page_table)
            so_p, st_p = s5_branch(up, jnp.zeros((x_prompt.shape[0], S5_GROUPS, S5_STATE, 2), state_s5.dtype),
                                   s5_lambda_re[e], s5_lambda_im[e], s5_log_dt[e], s5_b[e], s5_c[e],
                                   s5_d[e], s5_glu_w[e], s5_glu_b[e])
            so_s, st_s = s5_branch(us, state_s5[e], s5_lambda_re[e], s5_lambda_im[e], s5_log_dt[e],
                                   s5_b[e], s5_c[e], s5_d[e], s5_glu_w[e], s5_glu_b[e])
            yp = yp + jnp.concatenate([att_p * jax.nn.silu(gap), so_p * jax.nn.silu(gbp)], axis=-1) @ w_out_even[e]
            ys = ys + jnp.concatenate([att_s * jax.nn.silu(gas), so_s * jax.nn.silu(gbs)], axis=-1) @ w_out_even[e]
            k_p.append(kp); v_p.append(vp); k_s.append(kss); v_s.append(vss)
            s5_p.append(st_p); s5_s.append(st_s)
        else:
            o = i // 2
            out_p, buf_p, hn_p = odd_mix(
                hp, jnp.zeros((x_prompt.shape[0], SSD_CONV - 1, SSD_CONV_DIM), state_conv.dtype),
                jnp.zeros((x_prompt.shape[0], SSD_HEADS, SSD_HEAD_DIM, SSD_STATE), state_ssd.dtype),
                w_in_odd[o], conv_w[o], conv_b[o], ssd_dt_bias[o], ssd_a_log[o], ssd_d[o], ssd_norm_w[o], w_out_odd[o])
            out_s, buf_s, hn_s = odd_mix(
                hs, state_conv[o], state_ssd[o], w_in_odd[o], conv_w[o], conv_b[o], ssd_dt_bias[o],
                ssd_a_log[o], ssd_d[o], ssd_norm_w[o], w_out_odd[o])
            yp = yp + out_p
            ys = ys + out_s
            cv_p.append(buf_p); cv_s.append(buf_s); ssd_p.append(hn_p); ssd_s.append(hn_s)
    return (yp, ys, jnp.stack(k_p), jnp.stack(v_p), jnp.stack(k_s), jnp.stack(v_s),
            jnp.stack(s5_p), jnp.stack(s5_s), jnp.stack(cv_p), jnp.stack(cv_s),
            jnp.stack(ssd_p), jnp.stack(ssd_s))
```

```python
import functools

import jax
import jax.numpy as jnp
from jax import lax
from jax.experimental import pallas as pl
from jax.experimental.pallas import tpu as pltpu

F32 = jnp.float32
BF16 = jnp.bfloat16

HEAD_DIM = 128
MOBA_BLOCK = 256
MOBA_TOPK = 3
ROPE_THETA = 10000.0
S5_GROUP = 16
S5_STATE = 64
S5_SLAB_GROUPS = 8
SSD_HEAD_DIM = 64
SSD_GROUPS = 8
SSD_STATE = 128
SSD_CONV = 4
SSD_CHUNK = 128
EPS = 1e-6
NEG = -1e30
VMEM_LIMIT = 56 * 1024 * 1024


def _cparams(sem):
    return pltpu.CompilerParams(dimension_semantics=sem, vmem_limit_bytes=VMEM_LIMIT)


def _silu(x):
    return x * jax.nn.sigmoid(x)


def _split3(v):
    hi = v.astype(BF16)
    r1 = v - hi.astype(F32)
    mid = r1.astype(BF16)
    lo = (r1 - mid.astype(F32)).astype(BF16)
    return hi, mid, lo


def _dot3(v, m):
    hi, mid, lo = _split3(v)
    return (jnp.dot(hi, m, preferred_element_type=F32) + jnp.dot(mid, m, preferred_element_type=F32)
            + jnp.dot(lo, m, preferred_element_type=F32))


def _norm_inproj_kernel(*refs, rope_tiles, tn):
    if rope_tiles:
        x_ref, nw_ref, w_ref, cos_ref, sin_ref, qn_ref, kn_ref, o_ref, h_sc = refs
    else:
        x_ref, nw_ref, w_ref, o_ref, h_sc = refs
    j = pl.program_id(1)

    @pl.when(j == 0)
    def _():
        x = x_ref[...]
        ms = jnp.mean(x * x, axis=-1, keepdims=True)
        h_sc[...] = (x * lax.rsqrt(ms + EPS) * nw_ref[...]).astype(BF16)

    acc = jnp.dot(h_sc[...], w_ref[...], preferred_element_type=F32)
    if not rope_tiles:
        o_ref[...] = acc
        return

    @pl.when(j < 2 * rope_tiles)
    def _():
        hw = jnp.where(j < rope_tiles, qn_ref[...], kn_ref[...])
        cos = cos_ref[...]
        sin = sin_ref[...]
        for hh in range(tn // HEAD_DIM):
            sl = slice(hh * HEAD_DIM, (hh + 1) * HEAD_DIM)
            a = acc[:, sl]
            ms = jnp.mean(a * a, axis=-1, keepdims=True)
            a = a * lax.rsqrt(ms + EPS) * hw
            o_ref[:, sl] = a * cos + pltpu.roll(a, HEAD_DIM // 2, 1) * sin

    @pl.when(j >= 2 * rope_tiles)
    def _():
        o_ref[...] = acc


def _norm_inproj(x2d, nw, w, *, tm, tn, rope=None):
    n, d = x2d.shape
    nout = w.shape[1]
    grid = (n // tm, nout // tn)
    in_specs = [pl.BlockSpec((tm, d), lambda i, j: (i, 0)),
                pl.BlockSpec((1, d), lambda i, j: (0, 0)),
                pl.BlockSpec((d, tn), lambda i, j: (0, j))]
    args = [x2d, nw, w]
    rope_tiles = 0
    if rope is not None:
        cos2, sin2, qn, kn, attn_width = rope
        rope_tiles = attn_width // tn
        nper = cos2.shape[0] // tm
        in_specs += [pl.BlockSpec((tm, HEAD_DIM), lambda i, j: (i % nper, 0)),
                     pl.BlockSpec((tm, HEAD_DIM), lambda i, j: (i % nper, 0)),
                     pl.BlockSpec((1, HEAD_DIM), lambda i, j: (0, 0)),
                     pl.BlockSpec((1, HEAD_DIM), lambda i, j: (0, 0))]
        args += [cos2, sin2, qn, kn]
    return pl.pallas_call(
        functools.partial(_norm_inproj_kernel, rope_tiles=rope_tiles, tn=tn),
        out_shape=jax.ShapeDtypeStruct((n, nout), F32),
        grid=grid, in_specs=in_specs,
        out_specs=pl.BlockSpec((tm, tn), lambda i, j: (i, j)),
        scratch_shapes=[pltpu.VMEM((tm, d), BF16)],
        compiler_params=_cparams(("parallel", "arbitrary")),
        name="norm_inproj_rope" if rope_tiles else "norm_inproj",
    )(*args)


def _moba_prompt_kernel(q_ref, k_ref, v_ref, o_ref, kmean_sc, vt_sc, selb_sc, *, nb):
    qb = pl.program_id(2)
    blk = MOBA_BLOCK
    scale = HEAD_DIM ** -0.5

    @pl.when(qb == 0)
    def _():
        for n in range(nb):
            kb = k_ref[n * blk:(n + 1) * blk, :]
            kmean_sc[n:n + 1, :] = jnp.mean(kb, axis=0, keepdims=True)
            vt_sc[n] = v_ref[n * blk:(n + 1) * blk, :].T.astype(BF16)

    q = q_ref[...]
    q16 = q.astype(BF16)
    s_blk = lax.dot_general(kmean_sc[...], q, (((1,), (1,)), ((), ())),
                            precision=lax.Precision.HIGHEST, preferred_element_type=F32)
    n_iota = lax.broadcasted_iota(jnp.int32, s_blk.shape, 0)
    cnt = jnp.zeros(s_blk.shape, F32)
    for m in range(nb):
        s_m = s_blk[m:m + 1, :]
        beats = jnp.where(s_m > s_blk, 1.0, jnp.where((s_m == s_blk) & (n_iota > m), 1.0, 0.0))
        cnt = cnt + jnp.where(m < qb, beats, 0.0)
    sel = (n_iota < qb) & (cnt < float(MOBA_TOPK))
    selb_sc[...] = jnp.where(sel, 0.0, NEG)

    r0 = pl.multiple_of(qb * blk, blk)
    k_own = k_ref[pl.ds(r0, blk), :].astype(BF16)
    s = lax.dot_general(k_own, q16, (((1,), (1,)), ((), ())), preferred_element_type=F32) * scale
    key_i = lax.broadcasted_iota(jnp.int32, (blk, blk), 0)
    qry_i = lax.broadcasted_iota(jnp.int32, (blk, blk), 1)
    s = jnp.where(key_i <= qry_i, s, NEG)
    m0 = jnp.max(s, axis=0, keepdims=True)
    p = jnp.exp(s - m0)
    l0 = jnp.sum(p, axis=0, keepdims=True)
    acc0 = jnp.dot(vt_sc[qb], p.astype(BF16), preferred_element_type=F32)

    def body(kb, carry):
        m_prev, l_prev, acc = carry
        c0 = pl.multiple_of(kb * blk, blk)
        k_b = k_ref[pl.ds(c0, blk), :].astype(BF16)
        s = lax.dot_general(k_b, q16, (((1,), (1,)), ((), ())), preferred_element_type=F32) * scale
        s = s + selb_sc[pl.ds(kb, 1), :]
        m_new = jnp.maximum(m_prev, jnp.max(s, axis=0, keepdims=True))
        alpha = jnp.exp(m_prev - m_new)
        p = jnp.exp(s - m_new)
        l_new = alpha * l_prev + jnp.sum(p, axis=0, keepdims=True)
        acc = alpha * acc + jnp.dot(vt_sc[kb], p.astype(BF16), preferred_element_type=F32)
        return m_new, l_new, acc

    _, l_f, acc_f = lax.fori_loop(0, qb, body, (m0, l0, acc0))
    o_ref[...] = (acc_f / l_f).T


def _moba_prompt(proj, *, batch, seq, n_heads, q_col, k_col, v_col):
    nb = max(-(-seq // MOBA_BLOCK), MOBA_TOPK)
    assert seq % MOBA_BLOCK == 0 and nb * MOBA_BLOCK == seq
    nqb = seq // MOBA_BLOCK
    qc, kc, vc = q_col // HEAD_DIM, k_col // HEAD_DIM, v_col // HEAD_DIM
    return pl.pallas_call(
        functools.partial(_moba_prompt_kernel, nb=nb),
        out_shape=jax.ShapeDtypeStruct((batch * seq, n_heads * HEAD_DIM), F32),
        grid=(batch, n_heads, nqb),
        in_specs=[pl.BlockSpec((MOBA_BLOCK, HEAD_DIM), lambda b, h, i: (b * nqb + i, qc + h)),
                  pl.BlockSpec((seq, HEAD_DIM), lambda b, h, i: (b, kc + h)),
                  pl.BlockSpec((seq, HEAD_DIM), lambda b, h, i: (b, vc + h))],
        out_specs=pl.BlockSpec((MOBA_BLOCK, HEAD_DIM), lambda b, h, i: (b * nqb + i, h)),
        scratch_shapes=[pltpu.VMEM((nb, HEAD_DIM), F32),
                        pltpu.VMEM((nb, HEAD_DIM, MOBA_BLOCK), BF16),
                        pltpu.VMEM((nb, MOBA_BLOCK), F32)],
        compiler_params=_cparams(("parallel", "parallel", "arbitrary")),
        name="moba_prompt",
    )(proj, proj, proj)


def _moba_sample_kernel(pt_ref, q_ref, kn_ref, vn_ref, k0_ref, k1_ref, v0_ref, v1_ref, o_ref,
                        ksum_sc, m_sc, l_sc, acc_sc, *, nblk):
    del pt_ref
    j = pl.program_id(1)
    scale = HEAD_DIM ** -0.5
    q = q_ref[...]
    k0 = k0_ref[...]
    k1 = k1_ref[...]
    s0 = jnp.sum(k0 * q[None], axis=-1, keepdims=True) * scale
    s1 = jnp.sum(k1 * q[None], axis=-1, keepdims=True) * scale
    m = jnp.maximum(jnp.max(s0, axis=0), jnp.max(s1, axis=0))
    p0 = jnp.exp(s0 - m[None])
    p1 = jnp.exp(s1 - m[None])
    l = jnp.sum(p0, axis=0) + jnp.sum(p1, axis=0)
    acc = jnp.sum(p0 * v0_ref[...], axis=0) + jnp.sum(p1 * v1_ref[...], axis=0)
    ksum_sc[j] = jnp.sum(k0, axis=0) + jnp.sum(k1, axis=0)
    m_sc[j] = jnp.broadcast_to(m, acc.shape)
    l_sc[j] = jnp.broadcast_to(l, acc.shape)
    acc_sc[j] = acc

    @pl.when(j == nblk - 1)
    def _():
        kmean = ksum_sc[...] * (1.0 / MOBA_BLOCK)
        s_blk = jnp.sum(kmean * q[None], axis=-1, keepdims=True)
        n_iota = lax.broadcasted_iota(jnp.int32, s_blk.shape, 0)
        cnt = jnp.zeros(s_blk.shape, F32)
        for mm in range(nblk):
            s_m = s_blk[mm:mm + 1]
            cnt = cnt + jnp.where(s_m > s_blk, 1.0, jnp.where((s_m == s_blk) & (n_iota > mm), 1.0, 0.0))
        sel = cnt < float(MOBA_TOPK)
        m_b = m_sc[...]
        s_own = jnp.sum(q * kn_ref[...], axis=-1, keepdims=True) * scale
        m_tot = jnp.maximum(s_own, jnp.max(jnp.where(sel, m_b, NEG), axis=0))
        w_b = jnp.where(sel, jnp.exp(m_b - m_tot[None]), 0.0)
        w_own = jnp.exp(s_own - m_tot)
        num = w_own * vn_ref[...] + jnp.sum(w_b * acc_sc[...], axis=0)
        den = w_own + jnp.sum(w_b * l_sc[...], axis=0)
        o_ref[...] = num / den


def _moba_sample(q, k_new, v_new, cache_k, cache_v, page_table, layer):
    n, n_heads, _ = q.shape
    page = cache_k.shape[2]
    n_pages = page_table.shape[1]
    assert MOBA_BLOCK == 2 * page and (n_pages * page) % MOBA_BLOCK == 0
    nblk = n_pages * page // MOBA_BLOCK
    assert nblk >= MOBA_TOPK
    vec = pl.BlockSpec((None, n_heads, HEAD_DIM), lambda i, j, pt: (i, 0, 0))

    def page_spec(which):
        return pl.BlockSpec((None, None, page, n_heads, HEAD_DIM),
                            lambda i, j, pt: (layer, pt[i, 2 * j + which], 0, 0, 0))

    grid_spec = pltpu.PrefetchScalarGridSpec(
        num_scalar_prefetch=1, grid=(n, nblk),
        in_specs=[vec, vec, vec, page_spec(0), page_spec(1), page_spec(0), page_spec(1)],
        out_specs=pl.BlockSpec((None, n_heads, HEAD_DIM), lambda i, j, pt: (i, 0, 0)),
        scratch_shapes=[pltpu.VMEM((nblk, n_heads, HEAD_DIM), F32)] * 4)
    return pl.pallas_call(
        functools.partial(_moba_sample_kernel, nblk=nblk),
        out_shape=jax.ShapeDtypeStruct((n, n_heads, HEAD_DIM), F32),
        grid_spec=grid_spec,
        compiler_params=_cparams(("parallel", "arbitrary")),
        name="moba_sample",
    )(page_table, q, k_new, v_new, cache_k, cache_k, cache_v, cache_v)


def _s5_kernel(u_ref, x0_ref, bb_ref, cc_ref, ar_ref, ai_ref, d_ref, gw_ref, gb_ref,
               o_ref, st_ref, state_sc, xs_sc, y_sc, *, steps, nb, n_slabs):
    c = pl.program_id(0)
    half = S5_SLAB_GROUPS * S5_STATE
    wslab = S5_SLAB_GROUPS * S5_GROUP
    rows = steps * nb

    @pl.when(c == 0)
    def _():
        state_sc[...] = x0_ref[...]

    for gs in range(n_slabs):
        cols = slice(gs * wslab, (gs + 1) * wslab)
        ug = u_ref[:, :, cols].reshape(rows, wslab)
        xs_sc[...] = jnp.dot(ug.astype(BF16), bb_ref[gs], preferred_element_type=F32)
        ar = jnp.broadcast_to(ar_ref[gs], (nb, half))
        ai = jnp.broadcast_to(ai_ref[gs], (nb, half))

        def step(t, carry):
            xr, xi = carry
            r0 = pl.multiple_of(t * nb, nb)
            nxr = ar * xr - ai * xi + xs_sc[pl.ds(r0, nb), 0:half]
            nxi = ar * xi + ai * xr + xs_sc[pl.ds(r0, nb), half:2 * half]
            xs_sc[pl.ds(r0, nb), 0:half] = nxr
            xs_sc[pl.ds(r0, nb), half:2 * half] = nxi
            return nxr, nxi

        carry = (state_sc[gs, 0], state_sc[gs, 1])
        if steps == 1:
            xr, xi = step(0, carry)
        else:
            xr, xi = lax.fori_loop(0, steps, step, carry, unroll=4)
        state_sc[gs, 0] = xr
        state_sc[gs, 1] = xi
        y = jnp.dot(xs_sc[...].astype(BF16), cc_ref[gs], preferred_element_type=F32)
        y_sc[:, cols] = y + d_ref[:, cols] * ug

    z = jax.nn.gelu(y_sc[...])
    gate = jnp.dot(z.astype(BF16), gw_ref[...], preferred_element_type=F32) + gb_ref[...]
    o_ref[...] = (z * jax.nn.sigmoid(gate)).reshape(o_ref.shape)

    @pl.when(c == pl.num_programs(0) - 1)
    def _():
        st_ref[...] = state_sc[...]


def _s5(u_tb, x0, bb, cc, ar, ai, d, gw, gb, *, steps):
    t, nb, w = u_tb.shape
    n_slabs = bb.shape[0]
    half = S5_SLAB_GROUPS * S5_STATE
    rows = steps * nb
    const = lambda *shape: pl.BlockSpec(shape, lambda c: (0,) * len(shape))
    return pl.pallas_call(
        functools.partial(_s5_kernel, steps=steps, nb=nb, n_slabs=n_slabs),
        out_shape=(jax.ShapeDtypeStruct((t, nb, w), F32),
                   jax.ShapeDtypeStruct((n_slabs, 2, nb, half), F32)),
        grid=(t // steps,),
        in_specs=[pl.BlockSpec((steps, nb, w), lambda c: (c, 0, 0)),
                  const(n_slabs, 2, nb, half), const(*bb.shape), const(*cc.shape),
                  const(*ar.shape), const(*ai.shape), const(1, w), const(w, w), const(1, w)],
        out_specs=(pl.BlockSpec((steps, nb, w), lambda c: (c, 0, 0)), const(n_slabs, 2, nb, half)),
        scratch_shapes=[pltpu.VMEM((n_slabs, 2, nb, half), F32),
                        pltpu.VMEM((rows, 2 * half), F32),
                        pltpu.VMEM((rows, w), F32)],
        compiler_params=_cparams(("arbitrary",)),
        name="s5_scan",
    )(u_tb, x0, bb, cc, ar, ai, d, gw, gb)


def _s5_params(lam_re, lam_im, log_dt, b_ri, c_ri):
    g, p = lam_re.shape
    ns = g // S5_SLAB_GROUPS
    dt = jnp.exp(log_dt.astype(F32))[:, None]
    lr, li = lam_re.astype(F32), lam_im.astype(F32)
    mag = jnp.exp(lr * dt)
    ar, ai = mag * jnp.cos(li * dt), mag * jnp.sin(li * dt)
    d2 = lr * lr + li * li
    cr = ((ar - 1.0) * lr + ai * li) / d2
    ci = (ai * lr - (ar - 1.0) * li) / d2
    br, bi = b_ri[..., 0].astype(F32), b_ri[..., 1].astype(F32)
    bbr = cr[..., None] * br - ci[..., None] * bi
    bbi = cr[..., None] * bi + ci[..., None] * br
    eye = jnp.eye(S5_SLAB_GROUPS, dtype=F32)

    def blockdiag_in(m):
        m = m.reshape(ns, S5_SLAB_GROUPS, p, S5_GROUP)
        return jnp.einsum('sipc,ij->sicjp', m, eye).reshape(ns, S5_SLAB_GROUPS * S5_GROUP, S5_SLAB_GROUPS * p)

    def blockdiag_out(m):
        m = m.reshape(ns, S5_SLAB_GROUPS, S5_GROUP, p)
        return jnp.einsum('sicp,ij->sipjc', m, eye).reshape(ns, S5_SLAB_GROUPS * p, S5_SLAB_GROUPS * S5_GROUP)

    bb = jnp.concatenate([blockdiag_in(bbr), blockdiag_in(bbi)], axis=-1).astype(BF16)
    c_re, c_im = c_ri[..., 0].astype(F32), c_ri[..., 1].astype(F32)
    cc = jnp.concatenate([blockdiag_out(c_re), blockdiag_out(-c_im)], axis=1).astype(BF16)
    ar_s = ar.reshape(ns, 1, S5_SLAB_GROUPS * p)
    ai_s = ai.reshape(ns, 1, S5_SLAB_GROUPS * p)
    return bb, cc, ar_s, ai_s


def _s5_state_to_slabs(x0):
    n, g, p, _ = x0.shape
    ns = g // S5_SLAB_GROUPS
    return x0.reshape(n, ns, S5_SLAB_GROUPS * p, 2).transpose(1, 3, 0, 2)


def _s5_state_from_slabs(st, g):
    ns, _, n, hp = st.shape
    return st.transpose(2, 0, 3, 1).reshape(n, g, hp // S5_SLAB_GROUPS, 2)


def _outproj_even_kernel(att_ref, ga_ref, so_ref, gb_ref, w_ref, res_ref, o_ref, a_sc):
    j = pl.program_id(1)
    wa = att_ref.shape[1]

    @pl.when(j == 0)
    def _():
        a_sc[:, :wa] = (att_ref[...] * _silu(ga_ref[...])).astype(BF16)
        a_sc[:, wa:] = (so_ref[...] * _silu(gb_ref[...])).astype(BF16)

    o_ref[...] = res_ref[...] + jnp.dot(a_sc[...], w_ref[...], preferred_element_type=F32)


def _outproj_even(att, proj, so2d, so_index, w, res, *, ga_col, gb_col, tm, tn):
    n, wa = att.shape
    ws = so2d.shape[1] if so_index is None else wa
    d = w.shape[1]
    gac, gbc = ga_col // wa, gb_col // wa
    so_map = (lambda i, j: (i, 0)) if so_index is None else (lambda i, j: so_index(i))
    return pl.pallas_call(
        _outproj_even_kernel,
        out_shape=jax.ShapeDtypeStruct((n, d), F32),
        grid=(n // tm, d // tn),
        in_specs=[pl.BlockSpec((tm, wa), lambda i, j: (i, 0)),
                  pl.BlockSpec((tm, wa), lambda i, j: (i, gac)),
                  pl.BlockSpec((tm, ws), so_map),
                  pl.BlockSpec((tm, wa), lambda i, j: (i, gbc)),
                  pl.BlockSpec((wa + ws, tn), lambda i, j: (0, j)),
                  pl.BlockSpec((tm, tn), lambda i, j: (i, j))],
        out_specs=pl.BlockSpec((tm, tn), lambda i, j: (i, j)),
        scratch_shapes=[pltpu.VMEM((tm, wa + ws), BF16)],
        compiler_params=_cparams(("parallel", "arbitrary")),
        name="outproj_even",
    )(att, proj, so2d, proj, w, res)


def _matmul_res_kernel(a_ref, w_ref, res_ref, o_ref):
    o_ref[...] = res_ref[...] + jnp.dot(a_ref[...], w_ref[...], preferred_element_type=F32)


def _matmul_res(a, w, res, *, tm, tn):
    n, k = a.shape
    d = w.shape[1]
    return pl.pallas_call(
        _matmul_res_kernel,
        out_shape=jax.ShapeDtypeStruct((n, d), F32),
        grid=(n // tm, d // tn),
        in_specs=[pl.BlockSpec((tm, k), lambda i, j: (i, 0)),
                  pl.BlockSpec((k, tn), lambda i, j: (0, j)),
                  pl.BlockSpec((tm, tn), lambda i, j: (i, j))],
        out_specs=pl.BlockSpec((tm, tn), lambda i, j: (i, j)),
        compiler_params=_cparams(("parallel", "arbitrary")),
        name="outproj_odd",
    )(a, w, res)


def _softplus(x):
    return jnp.maximum(x, 0.0) + jnp.log1p(jnp.exp(-jnp.abs(x)))


def _ssd_prompt_kernel(z_ref, x_ref, b_ref, c_ref, dt_ref, cw_ref, cb_ref, dtb_ref, alog_ref, de_ref,
                       nw_ref, e_ref, cum_ref, y_ref, st_ref, xpad_sc, act_sc, ht_sc, y_sc, *, chunk, width):
    c = pl.program_id(1)
    L = chunk
    hist = 8
    gw = width // SSD_GROUPS
    hpg = gw // SSD_HEAD_DIM
    bw = SSD_GROUPS * SSD_STATE

    @pl.when(c == 0)
    def _():
        xpad_sc[0:hist, :] = jnp.zeros((hist, xpad_sc.shape[1]), F32)
        ht_sc[...] = jnp.zeros(ht_sc.shape, F32)

    xpad_sc[hist:hist + L, 0:width] = x_ref[...]
    xpad_sc[hist:hist + L, width:width + bw] = b_ref[...]
    xpad_sc[hist:hist + L, width + bw:width + 2 * bw] = c_ref[...]
    cs = 512
    for j in range(xpad_sc.shape[1] // cs):
        sl = slice(j * cs, (j + 1) * cs)
        conv = cb_ref[:, sl]
        for k in range(SSD_CONV):
            off = hist - (SSD_CONV - 1) + k
            conv = conv + cw_ref[k:k + 1, sl] * xpad_sc[off:off + L, sl]
        act_sc[:, sl] = _silu(conv)
    xpad_sc[0:hist, :] = xpad_sc[L:L + hist, :]

    dt = _softplus(dt_ref[...] + dtb_ref[...])
    a = -jnp.exp(alog_ref[...])
    adt = dt * a
    a_cs_t = _dot3(adt.T, cum_ref[...])
    a_cs = a_cs_t.T
    ea = jnp.exp(a_cs)
    ds = jnp.exp(a_cs[L - 1:L, :] - a_cs)
    row_i = lax.broadcasted_iota(jnp.int32, (L, L), 0)
    col_i = lax.broadcasted_iota(jnp.int32, (L, L), 1)
    tri = row_i >= col_i

    for g in range(SSD_GROUPS):
        gsl = slice(g * gw, (g + 1) * gw)
        e_g = e_ref[:, gsl]
        dt_g = _dot3(dt, e_g)
        ea_g = _dot3(ea, e_g)
        ds_g = _dot3(ds, e_g)
        xs_g = act_sc[:, gsl]
        xdt_g = xs_g * dt_g
        b_g = act_sc[:, width + g * SSD_STATE:width + (g + 1) * SSD_STATE].astype(BF16)
        c_g = act_sc[:, width + bw + g * SSD_STATE:width + bw + (g + 1) * SSD_STATE].astype(BF16)
        cb = lax.dot_general(c_g, b_g, (((1,), (1,)), ((), ())), preferred_element_type=F32)
        h_prev = ht_sc[g]
        y_off = jnp.dot(c_g, h_prev.astype(BF16), preferred_element_type=F32) * ea_g
        states_t = lax.dot_general(b_g, (xdt_g * ds_g).astype(BF16), (((0,), (0,)), ((), ())),
                                   preferred_element_type=F32)
        ht_sc[g] = h_prev * ea_g[L - 1:L, :] + states_t
        xdt16 = xdt_g.astype(BF16)
        for r in range(hpg):
            h = g * hpg + r
            hsl = slice(r * SSD_HEAD_DIM, (r + 1) * SSD_HEAD_DIM)
            diff = a_cs[:, h:h + 1] - a_cs_t[h:h + 1, :]
            lm = jnp.exp(jnp.where(tri, diff, NEG))
            yd = jnp.dot((cb * lm).astype(BF16), xdt16[:, hsl], preferred_element_type=F32)
            y_sc[:, hsl] = yd + y_off[:, hsl] + de_ref[:, g * gw + r * SSD_HEAD_DIM:g * gw + (r + 1) * SSD_HEAD_DIM] * xs_g[:, hsl]
        y = y_sc[...] * _silu(z_ref[:, gsl])
        ms = jnp.mean(y * y, axis=-1, keepdims=True)
        y_ref[:, gsl] = (y * lax.rsqrt(ms + EPS) * nw_ref[:, gsl]).astype(y_ref.dtype)

    @pl.when(c == pl.num_programs(1) - 1)
    def _():
        for g in range(SSD_GROUPS):
            st_ref[g * gw:(g + 1) * gw, :] = ht_sc[g].T


def _ssd_prompt(proj, cw, cb, dtb, alog, de, nw, e_mat, tril, *, batch, seq, width):
    L = SSD_CHUNK
    assert seq % L == 0
    nc = seq // L
    bw = SSD_GROUPS * SSD_STATE
    conv_dim = width + 2 * bw
    xcol = width // width
    bcol = (2 * width) // bw
    ccol = (2 * width + bw) // bw
    dcol = (2 * width + 2 * bw) // 128
    const = lambda *shape: pl.BlockSpec(shape, lambda b, c: (0,) * len(shape))
    return pl.pallas_call(
        functools.partial(_ssd_prompt_kernel, chunk=L, width=width),
        out_shape=(jax.ShapeDtypeStruct((batch * seq, width), BF16),
                   jax.ShapeDtypeStruct((batch, width, SSD_STATE), F32)),
        grid=(batch, nc),
        in_specs=[pl.BlockSpec((L, width), lambda b, c: (b * nc + c, 0)),
                  pl.BlockSpec((L, width), lambda b, c: (b * nc + c, xcol)),
                  pl.BlockSpec((L, bw), lambda b, c: (b * nc + c, bcol)),
                  pl.BlockSpec((L, bw), lambda b, c: (b * nc + c, ccol)),
                  pl.BlockSpec((L, 128), lambda b, c: (b * nc + c, dcol)),
                  const(SSD_CONV, conv_dim), const(1, conv_dim), const(1, 128), const(1, 128),
                  const(1, width), const(1, width), const(128, width), const(L, L)],
        out_specs=(pl.BlockSpec((L, width), lambda b, c: (b * nc + c, 0)),
                   pl.BlockSpec((None, width, SSD_STATE), lambda b, c: (b, 0, 0))),
        scratch_shapes=[pltpu.VMEM((L + 8, conv_dim), F32),
                        pltpu.VMEM((L, conv_dim), F32),
                        pltpu.VMEM((SSD_GROUPS, SSD_STATE, width // SSD_GROUPS), F32),
                        pltpu.VMEM((L, width // SSD_GROUPS), F32)],
        compiler_params=_cparams(("parallel", "arbitrary")),
        name="ssd_prompt",
    )(proj, proj, proj, proj, proj, cw, cb, dtb, alog, de, nw, e_mat, tril)


def _conv_step_kernel(x_ref, s0_ref, s1_ref, s2_ref, cw_ref, cb_ref, dt_ref, dtb_ref, alog_ref,
                      act_ref, dto_ref, dao_ref):
    conv = (cb_ref[...] + cw_ref[0:1, :] * s0_ref[...] + cw_ref[1:2, :] * s1_ref[...]
            + cw_ref[2:3, :] * s2_ref[...] + cw_ref[3:4, :] * x_ref[...])
    act_ref[...] = _silu(conv)

    @pl.when(pl.program_id(0) == 0)
    def _():
        dt = _softplus(dt_ref[...] + dtb_ref[...])
        dto_ref[...] = dt
        dao_ref[...] = jnp.exp(dt * -jnp.exp(alog_ref[...]))


def _conv_step(proj, conv_state2d, cw, cb, dtb, alog, *, width):
    n = proj.shape[0]
    conv_dim = cw.shape[1]
    cs = 512
    nj = conv_dim // cs
    x0 = width // cs
    dcol = (width + conv_dim) // 128
    return pl.pallas_call(
        _conv_step_kernel,
        out_shape=(jax.ShapeDtypeStruct((n, conv_dim), F32), jax.ShapeDtypeStruct((n, 128), F32),
                   jax.ShapeDtypeStruct((n, 128), F32)),
        grid=(nj,),
        in_specs=[pl.BlockSpec((n, cs), lambda j: (0, x0 + j)),
                  pl.BlockSpec((n, cs), lambda j: (0, j)),
                  pl.BlockSpec((n, cs), lambda j: (0, nj + j)),
                  pl.BlockSpec((n, cs), lambda j: (0, 2 * nj + j)),
                  pl.BlockSpec((SSD_CONV, cs), lambda j: (0, j)),
                  pl.BlockSpec((1, cs), lambda j: (0, j)),
                  pl.BlockSpec((n, 128), lambda j: (0, dcol)),
                  pl.BlockSpec((1, 128), lambda j: (0, 0)),
                  pl.BlockSpec((1, 128), lambda j: (0, 0))],
        out_specs=(pl.BlockSpec((n, cs), lambda j: (0, j)),
                   pl.BlockSpec((n, 128), lambda j: (0, 0)),
                   pl.BlockSpec((n, 128), lambda j: (0, 0))),
        compiler_params=_cparams(("arbitrary",)),
        name="ssd_conv_step",
    )(proj, conv_state2d, conv_state2d, conv_state2d, cw, cb, proj, dtb, alog)


def _ssd_step_kernel(dt_sm, da_sm, h0_ref, xt_ref, b_ref, c_ref, z_ref, x_ref, de_ref, nw_ref,
                     hn_ref, y_ref, *, width):
    n = pl.program_id(0)
    gw = width // SSD_GROUPS
    hpg = gw // SSD_HEAD_DIM
    nrow = xt_ref.shape[1]
    rowsel = lax.broadcasted_iota(jnp.int32, (nrow, SSD_STATE), 0) == n
    for g in range(SSD_GROUPS):
        ssl = slice(g * SSD_STATE, (g + 1) * SSD_STATE)
        gsl = slice(g * gw, (g + 1) * gw)
        rhs = jnp.where(rowsel, b_ref[:, ssl], 0.0).astype(BF16)
        outer = jnp.dot(xt_ref[gsl, :], rhs, preferred_element_type=F32)
        for r in range(hpg):
            h = g * hpg + r
            rsl = slice(g * gw + r * SSD_HEAD_DIM, g * gw + (r + 1) * SSD_HEAD_DIM)
            hn_ref[rsl, :] = (da_sm[n, h] * h0_ref[rsl, :]
                              + dt_sm[n, h] * outer[r * SSD_HEAD_DIM:(r + 1) * SSD_HEAD_DIM, :])
        c8 = jnp.broadcast_to(c_ref[:, ssl], (8, SSD_STATE)).astype(BF16)
        yg = lax.dot_general(c8, hn_ref[gsl, :].astype(BF16), (((1,), (1,)), ((), ())),
                             preferred_element_type=F32)[0:1, :]
        y = (yg + de_ref[:, gsl] * x_ref[:, gsl]) * _silu(z_ref[:, gsl])
        ms = jnp.mean(y * y, axis=-1, keepdims=True)
        y_ref[:, gsl] = y * lax.rsqrt(ms + EPS) * nw_ref[:, gsl]


def _ssd_step(dt, da, h0, xt, bm, cm, z, xs, de, nw, *, width):
    n = h0.shape[0]
    bw = SSD_GROUPS * SSD_STATE
    row = lambda w: pl.BlockSpec((None, 1, w), lambda i: (i, 0, 0))
    smem = pl.BlockSpec(memory_space=pltpu.SMEM)
    return pl.pallas_call(
        functools.partial(_ssd_step_kernel, width=width),
        out_shape=(jax.ShapeDtypeStruct(h0.shape, F32), jax.ShapeDtypeStruct((n, 1, width), F32)),
        grid=(n,),
        in_specs=[smem, smem,
                  pl.BlockSpec((None, width, SSD_STATE), lambda i: (i, 0, 0)),
                  pl.BlockSpec((width, n), lambda i: (0, 0)),
                  row(bw), row(bw), row(width), row(width),
                  pl.BlockSpec((1, width), lambda i: (0, 0)),
                  pl.BlockSpec((1, width), lambda i: (0, 0))],
        out_specs=(pl.BlockSpec((None, width, SSD_STATE), lambda i: (i, 0, 0)), row(width)),
        compiler_params=_cparams(("arbitrary",)),
        name="ssd_step",
    )(dt, da, h0, xt, bm, cm, z, xs, de, nw)


def _rope_tables(pos):
    half = HEAD_DIM // 2
    inv = ROPE_THETA ** (-jnp.arange(half, dtype=F32) / half)
    ang = pos.astype(F32)[:, None] * inv[None, :]
    cos, sin = jnp.cos(ang), jnp.sin(ang)
    return jnp.concatenate([cos, cos], axis=-1), jnp.concatenate([-sin, sin], axis=-1)


def _even_layer(yp, ys, e, norm_w_i, cache_k, cache_v, page_table, state_s5, w_in, qn, kn,
                lam_re, lam_im, log_dt, s5_b, s5_c, s5_d, glu_w, glu_b, w_out):
    batch, seq, d = yp.shape
    nsamp = ys.shape[0]
    assert ys.shape[1] == 1
    aw = d // 2
    n_heads = aw // HEAD_DIM
    n_past = page_table.shape[1] * cache_k.shape[2]
    groups = lam_re.shape[0]

    w_in16 = w_in.astype(BF16)
    w_out16 = w_out.astype(BF16)
    nw = norm_w_i.reshape(1, d)
    qn2, kn2 = qn.reshape(1, HEAD_DIM), kn.reshape(1, HEAD_DIM)
    cos_p, sin_p = _rope_tables(jnp.arange(seq, dtype=jnp.int32))
    cos_s, sin_s = _rope_tables(jnp.full((nsamp,), n_past, dtype=jnp.int32))

    xp2 = yp.reshape(batch * seq, d)
    xs2 = ys.reshape(nsamp, d)
    proj_p = _norm_inproj(xp2, nw, w_in16, tm=min(1024, seq), tn=512, rope=(cos_p, sin_p, qn2, kn2, aw))
    proj_s = _norm_inproj(xs2, nw, w_in16, tm=nsamp, tn=512, rope=(cos_s, sin_s, qn2, kn2, aw))

    att_p = _moba_prompt(proj_p, batch=batch, seq=seq, n_heads=n_heads, q_col=0, k_col=aw, v_col=2 * aw)
    k_s = proj_s[:, aw:2 * aw].reshape(nsamp, n_heads, HEAD_DIM)
    v_s = proj_s[:, 2 * aw:3 * aw].reshape(nsamp, n_heads, HEAD_DIM)
    q_s = proj_s[:, 0:aw].reshape(nsamp, n_heads, HEAD_DIM)
    att_s = _moba_sample(q_s, k_s, v_s, cache_k, cache_v, page_table, e).reshape(nsamp, aw)

    bb, cc, ar, ai = _s5_params(lam_re, lam_im, log_dt, s5_b, s5_c)
    d2 = s5_d.reshape(1, aw).astype(F32)
    gw16 = glu_w.astype(BF16)
    gb2 = glu_b.reshape(1, aw).astype(F32)
    u_p = proj_p[:, 4 * aw:5 * aw].reshape(batch, seq, aw).transpose(1, 0, 2)
    zero_state = jnp.zeros((groups // S5_SLAB_GROUPS, 2, batch, S5_SLAB_GROUPS * S5_STATE), F32)
    so_p, st_p = _s5(u_p, zero_state, bb, cc, ar, ai, d2, gw16, gb2, steps=64)
    u_s = proj_s[:, 4 * aw:5 * aw].reshape(1, nsamp, aw)
    so_s, st_s = _s5(u_s, _s5_state_to_slabs(state_s5), bb, cc, ar, ai, d2, gw16, gb2, steps=1)

    tm_p = 512
    nt = seq // tm_p
    yp_new = _outproj_even(att_p, proj_p, so_p.reshape(seq, batch * aw), lambda i: (i % nt, i // nt),
                           w_out16, xp2, ga_col=3 * aw, gb_col=5 * aw, tm=tm_p, tn=512)
    ys_new = _outproj_even(att_s, proj_s, so_s.reshape(nsamp, aw), None, w_out16, xs2,
                           ga_col=3 * aw, gb_col=5 * aw, tm=nsamp, tn=512)

    k_p = proj_p[:, aw:2 * aw].reshape(batch, seq, n_heads, HEAD_DIM)
    v_p = proj_p[:, 2 * aw:3 * aw].reshape(batch, seq, n_heads, HEAD_DIM)
    outs = (k_p, v_p, k_s.reshape(nsamp, 1, n_heads, HEAD_DIM), v_s.reshape(nsamp, 1, n_heads, HEAD_DIM),
            _s5_state_from_slabs(st_p, groups), _s5_state_from_slabs(st_s, groups))
    return yp_new.reshape(batch, seq, d), ys_new.reshape(nsamp, 1, d), outs


def _odd_layer(yp, ys, norm_w_i, state_conv, state_ssd, w_in, conv_w, conv_b, dt_bias, a_log, d_skip,
               ssd_norm_w, w_out):
    batch, seq, d = yp.shape
    nsamp = ys.shape[0]
    n_heads = a_log.shape[0]
    width = n_heads * SSD_HEAD_DIM
    bw = SSD_GROUPS * SSD_STATE
    conv_dim = width + 2 * bw
    odd_in = w_in.shape[1]
    assert odd_in == width + conv_dim + n_heads and n_heads <= 128
    tn = 512
    nout = -(-(width + conv_dim + 128) // tn) * tn

    w_in16 = jnp.pad(w_in, ((0, 0), (0, nout - odd_in))).astype(BF16)
    w_out16 = w_out.astype(BF16)
    nw = norm_w_i.reshape(1, d)
    pad128 = lambda v: jnp.pad(v.astype(F32), (0, 128 - n_heads)).reshape(1, 128)
    dtb, alog = pad128(dt_bias), pad128(a_log)
    de = jnp.repeat(d_skip.astype(F32), SSD_HEAD_DIM).reshape(1, width)
    gnw = ssd_norm_w.reshape(1, width).astype(F32)
    cb = conv_b.reshape(1, conv_dim).astype(F32)
    cw = conv_w.astype(F32)
    e_mat = (jnp.arange(128)[:, None] == (jnp.arange(width)[None, :] // SSD_HEAD_DIM)).astype(BF16)
    tril = (jnp.arange(SSD_CHUNK)[:, None] <= jnp.arange(SSD_CHUNK)[None, :]).astype(BF16)

    xp2 = yp.reshape(batch * seq, d)
    xs2 = ys.reshape(nsamp, d)
    proj_p = _norm_inproj(xp2, nw, w_in16, tm=min(1024, seq), tn=tn)
    proj_s = _norm_inproj(xs2, nw, w_in16, tm=nsamp, tn=tn)

    yn_p, st_p = _ssd_prompt(proj_p, cw, cb, dtb, alog, de, gnw, e_mat, tril, batch=batch, seq=seq, width=width)
    yp_new = _matmul_res(yn_p, w_out16, xp2, tm=512, tn=512)

    act_s, dt_s, da_s = _conv_step(proj_s, state_conv.reshape(nsamp, (SSD_CONV - 1) * conv_dim), cw, cb, dtb, alog,
                                   width=width)
    xs_s = act_s[:, :width]
    hn_s, yn_s = _ssd_step(dt_s, da_s, state_ssd.reshape(nsamp, width, SSD_STATE), xs_s.T.astype(BF16),
                           act_s[:, width:width + bw].reshape(nsamp, 1, bw),
                           act_s[:, width + bw:].reshape(nsamp, 1, bw),
                           proj_s[:, :width].reshape(nsamp, 1, width), xs_s.reshape(nsamp, 1, width),
                           de, gnw, width=width)
    ys_new = _matmul_res(yn_s.reshape(nsamp, width).astype(BF16), w_out16, xs2, tm=nsamp, tn=512)

    xbc_p = proj_p[:, width:width + conv_dim].reshape(batch, seq, conv_dim)
    buf_p = xbc_p[:, seq - (SSD_CONV - 1):, :]
    buf_s = jnp.concatenate([state_conv[:, 1:, :], proj_s[:, width:width + conv_dim].reshape(nsamp, 1, conv_dim)], axis=1)
    outs = (buf_p, buf_s, st_p.reshape(batch, n_heads, SSD_HEAD_DIM, SSD_STATE),
            hn_s.reshape(nsamp, n_heads, SSD_HEAD_DIM, SSD_STATE))
    return yp_new.reshape(batch, seq, d), ys_new.reshape(nsamp, 1, d), outs


def kernel(x_prompt, x_sample, cache_k, cache_v, page_table, state_s5, state_conv, state_ssd, norm_w, w_in_even, q_norm_w, k_norm_w, s5_lambda_re, s5_lambda_im, s5_log_dt, s5_b, s5_c, s5_d, s5_glu_w, s5_glu_b, w_out_even, w_in_odd, conv_w, conv_b, ssd_dt_bias, ssd_a_log, ssd_d, ssd_norm_w, w_out_odd):
    depth = norm_w.shape[0]
    yp, ys = x_prompt, x_sample
    even_outs, odd_outs = [], []
    for i in range(depth):
        if i % 2 == 0:
            e = i // 2
            yp, ys, outs = _even_layer(yp, ys, e, norm_w[i], cache_k, cache_v, page_table, state_s5[e],
                                       w_in_even[e], q_norm_w[e], k_norm_w[e], s5_lambda_re[e], s5_lambda_im[e],
                                       s5_log_dt[e], s5_b[e], s5_c[e], s5_d[e], s5_glu_w[e], s5_glu_b[e], w_out_even[e])
            even_outs.append(outs)
        else:
            o = i // 2
            yp, ys, outs = _odd_layer(yp, ys, norm_w[i], state_conv[o], state_ssd[o], w_in_odd[o], conv_w[o],
                                      conv_b[o], ssd_dt_bias[o], ssd_a_log[o], ssd_d[o], ssd_norm_w[o], w_out_odd[o])
            odd_outs.append(outs)
    stack = lambda outs, k: jnp.stack([o[k] for o in outs])
    return (yp, ys, stack(even_outs, 0), stack(even_outs, 1), stack(even_outs, 2), stack(even_outs, 3),
            stack(even_outs, 4), stack(even_outs, 5), stack(odd_outs, 0), stack(odd_outs, 1),
            stack(odd_outs, 2), stack(odd_outs, 3))
```

```python
import functools

import jax
import jax.numpy as jnp
from jax import lax
from jax.experimental import pallas as pl
from jax.experimental.pallas import tpu as pltpu

F32 = jnp.float32
BF16 = jnp.bfloat16

HEAD_DIM = 128
MOBA_BLOCK = 256
MOBA_TOPK = 3
ROPE_THETA = 10000.0
S5_GROUP = 16
S5_STATE = 64
S5_SLAB_GROUPS = 8
SSD_HEAD_DIM = 64
SSD_GROUPS = 8
SSD_STATE = 128
SSD_CONV = 4
SSD_CHUNK = 128
EPS = 1e-6
NEG = -1e30
VMEM_LIMIT = 56 * 1024 * 1024


def _cparams(sem):
    return pltpu.CompilerParams(dimension_semantics=sem, vmem_limit_bytes=VMEM_LIMIT)


def _silu(x):
    hx = 0.5 * x
    return hx + hx * jnp.tanh(hx)


def _split3(v):
    hi = v.astype(BF16)
    r1 = v - hi.astype(F32)
    mid = r1.astype(BF16)
    lo = (r1 - mid.astype(F32)).astype(BF16)
    return hi, mid, lo


def _dot3(v, m):
    hi, mid, lo = _split3(v)
    return (jnp.dot(hi, m, preferred_element_type=F32) + jnp.dot(mid, m, preferred_element_type=F32)
            + jnp.dot(lo, m, preferred_element_type=F32))


def _norm_inproj_kernel(*refs, routes, has_rope, tn):
    n_out = 1 + max(r[0] for r in routes)
    ins, outs, h_sc = refs[:-(n_out + 1)], refs[-(n_out + 1):-1], refs[-1]
    if has_rope:
        x_ref, nw_ref, w_ref, cos_ref, sin_ref, qn_ref, kn_ref = ins
    else:
        x_ref, nw_ref, w_ref = ins
    j = pl.program_id(1)

    @pl.when(j == 0)
    def _():
        x = x_ref[...]
        ms = jnp.mean(x * x, axis=-1, keepdims=True)
        h_sc[...] = (x * lax.rsqrt(ms + EPS) * nw_ref[...]).astype(BF16)

    acc = jnp.dot(h_sc[...], w_ref[...], preferred_element_type=F32)
    for out_idx, j0, j1, kind in routes:
        o_ref = outs[out_idx]

        @pl.when((j >= j0) & (j < j1))
        def _(o_ref=o_ref, kind=kind):
            if kind == "plain":
                o_ref[...] = acc
                return
            hw = qn_ref[...] if kind == "q" else kn_ref[...]
            cos = cos_ref[...]
            sin = sin_ref[...]
            for hh in range(tn // HEAD_DIM):
                sl = slice(hh * HEAD_DIM, (hh + 1) * HEAD_DIM)
                a = acc[:, sl]
                ms = jnp.mean(a * a, axis=-1, keepdims=True)
                a = a * lax.rsqrt(ms + EPS) * hw
                o_ref[:, sl] = a * cos + pltpu.roll(a, HEAD_DIM // 2, 1) * sin


def _owned_col(j, ranges):
    total = sum(j1 - j0 for j0, j1 in ranges)
    cnt = sum(jnp.clip(j - j0, 0, j1 - j0) for j0, j1 in ranges)
    return jnp.minimum(cnt, total - 1)


def _norm_inproj(x2d, nw, w, *, tm, tn, routes, outs, rope=None):
    n, d = x2d.shape
    nout = w.shape[1]
    grid = (n // tm, nout // tn)
    in_specs = [pl.BlockSpec((tm, d), lambda i, j: (i, 0)),
                pl.BlockSpec((1, d), lambda i, j: (0, 0)),
                pl.BlockSpec((d, tn), lambda i, j: (0, j))]
    args = [x2d, nw, w]
    if rope is not None:
        cos2, sin2, qn, kn = rope
        nper = cos2.shape[0] // tm
        in_specs += [pl.BlockSpec((tm, HEAD_DIM), lambda i, j: (i % nper, 0)),
                     pl.BlockSpec((tm, HEAD_DIM), lambda i, j: (i % nper, 0)),
                     pl.BlockSpec((1, HEAD_DIM), lambda i, j: (0, 0)),
                     pl.BlockSpec((1, HEAD_DIM), lambda i, j: (0, 0))]
        args += [cos2, sin2, qn, kn]
    out_shapes, out_specs = [], []
    for k, (shape, block_map) in enumerate(outs):
        ranges = [(j0, j1) for idx, j0, j1, _ in routes if idx == k]
        out_shapes.append(jax.ShapeDtypeStruct(shape, F32))
        out_specs.append(pl.BlockSpec(
            (tm, tn), lambda i, j, ranges=ranges, block_map=block_map: block_map(i, _owned_col(j, ranges))))
    res = pl.pallas_call(
        functools.partial(_norm_inproj_kernel, routes=tuple(routes), has_rope=rope is not None, tn=tn),
        out_shape=tuple(out_shapes),
        grid=grid, in_specs=in_specs,
        out_specs=tuple(out_specs),
        scratch_shapes=[pltpu.VMEM((tm, d), BF16)],
        compiler_params=_cparams(("parallel", "arbitrary")),
        name="norm_inproj_rope" if rope is not None else "norm_inproj",
    )(*args)
    return res


MOBA_HEADS_PER_STEP = 4


def _moba_prompt_kernel(q_ref, k_ref, v_ref, o_ref, kmean_sc, k16_sc, vt_sc, *, nb):
    qb = pl.program_id(2)
    blk = MOBA_BLOCK
    scale = HEAD_DIM ** -0.5
    hps = MOBA_HEADS_PER_STEP
    hcol = lambda hh: slice(hh * HEAD_DIM, (hh + 1) * HEAD_DIM)

    @pl.when(qb == 0)
    def _():
        for hh in range(hps):
            for n in range(nb):
                kb = k_ref[n * blk:(n + 1) * blk, hcol(hh)]
                kmean_sc[hh, n:n + 1, :] = jnp.mean(kb, axis=0, keepdims=True)
                k16_sc[hh, n * blk:(n + 1) * blk, :] = kb.astype(BF16)
                vt_sc[hh, :, n * blk:(n + 1) * blk] = v_ref[n * blk:(n + 1) * blk, hcol(hh)].T.astype(BF16)

    key_i = lax.broadcasted_iota(jnp.int32, (blk, blk), 0)
    qry_i = lax.broadcasted_iota(jnp.int32, (blk, blk), 1)
    causal = key_i <= qry_i
    q16s, selbs = [], []
    for hh in range(hps):
        q = q_ref[:, hcol(hh)]
        q16s.append(q.astype(BF16))
        s_blk = lax.dot_general(kmean_sc[hh], q, (((1,), (1,)), ((), ())),
                                precision=lax.Precision.HIGHEST, preferred_element_type=F32)
        n_iota = lax.broadcasted_iota(jnp.int32, s_blk.shape, 0)
        cnt = jnp.zeros(s_blk.shape, F32)
        for m in range(nb):
            s_m = s_blk[m:m + 1, :]
            beats = jnp.where(s_m > s_blk, 1.0, jnp.where((s_m == s_blk) & (n_iota > m), 1.0, 0.0))
            cnt = cnt + jnp.where(m < qb, beats, 0.0)
        sel = (n_iota < qb) & (cnt < float(MOBA_TOPK))
        selbs.append(jnp.where(sel, 0.0, NEG))

    for own in range(nb):
        @pl.when(qb == own)
        def _(own=own):
            for hh in range(hps):
                scores = []
                for kb in range(own + 1):
                    s = lax.dot_general(k16_sc[hh, kb * blk:(kb + 1) * blk, :], q16s[hh], (((1,), (1,)), ((), ())),
                                        preferred_element_type=F32) * scale
                    scores.append(jnp.where(causal, s, NEG) if kb == own else s + selbs[hh][kb:kb + 1, :])
                m = functools.reduce(jnp.maximum, [jnp.max(s, axis=0, keepdims=True) for s in scores])
                l = jnp.zeros_like(m)
                acc = jnp.zeros((HEAD_DIM, blk), F32)
                for kb, s in enumerate(scores):
                    p = jnp.exp(s - m)
                    l = l + jnp.sum(p, axis=0, keepdims=True)
                    acc = acc + jnp.dot(vt_sc[hh, :, kb * blk:(kb + 1) * blk], p.astype(BF16),
                                        preferred_element_type=F32)
                o_ref[:, hcol(hh)] = (acc / l).T


def _moba_prompt(q_arr, k_arr, v_arr, *, batch, seq, n_heads):
    nb = max(-(-seq // MOBA_BLOCK), MOBA_TOPK)
    assert seq % MOBA_BLOCK == 0 and nb * MOBA_BLOCK == seq
    hps = MOBA_HEADS_PER_STEP
    assert n_heads % hps == 0
    nqb = seq // MOBA_BLOCK
    cw = hps * HEAD_DIM
    return pl.pallas_call(
        functools.partial(_moba_prompt_kernel, nb=nb),
        out_shape=jax.ShapeDtypeStruct((batch * seq, n_heads * HEAD_DIM), F32),
        grid=(batch, n_heads // hps, nqb),
        in_specs=[pl.BlockSpec((MOBA_BLOCK, cw), lambda b, h, i: (b * nqb + i, h)),
                  pl.BlockSpec((seq, cw), lambda b, h, i: (b, h)),
                  pl.BlockSpec((seq, cw), lambda b, h, i: (b, h))],
        out_specs=pl.BlockSpec((MOBA_BLOCK, cw), lambda b, h, i: (b * nqb + i, h)),
        scratch_shapes=[pltpu.VMEM((hps, nb, HEAD_DIM), F32),
                        pltpu.VMEM((hps, seq, HEAD_DIM), BF16),
                        pltpu.VMEM((hps, HEAD_DIM, seq), BF16)],
        compiler_params=_cparams(("parallel", "parallel", "arbitrary")),
        name="moba_prompt",
    )(q_arr, k_arr, v_arr)


def _moba_sample_kernel(*refs, nblk, bps, ppb):
    npg = bps * ppb
    q_ref, kn_ref, vn_ref = refs[1:4]
    k_refs, v_refs = refs[4:4 + npg], refs[4 + npg:4 + 2 * npg]
    o_ref, ksum_sc, m_sc, l_sc, acc_sc = refs[4 + 2 * npg:]
    j = pl.program_id(1)
    scale = HEAD_DIM ** -0.5
    q = q_ref[...]
    q16 = q.astype(BF16)
    n_heads = q.shape[0]
    rows = k_refs[0].shape[0] * n_heads
    lane_head = lax.broadcasted_iota(jnp.int32, (n_heads, rows), 1) % n_heads
    mine = lane_head == lax.broadcasted_iota(jnp.int32, (n_heads, rows), 0)
    for b in range(bps):
        scores, ksum = [], None
        for k_ref in k_refs[b * ppb:(b + 1) * ppb]:
            kp = k_ref[...]
            kpsum = jnp.sum(kp, axis=0)
            ksum = kpsum if ksum is None else ksum + kpsum
            s = lax.dot_general(q16, kp.reshape(rows, HEAD_DIM).astype(BF16), (((1,), (1,)), ((), ())),
                                preferred_element_type=F32) * scale
            scores.append(jnp.where(mine, s, NEG))
        m = functools.reduce(jnp.maximum, [jnp.max(s, axis=1, keepdims=True) for s in scores])
        l = jnp.zeros_like(m)
        acc = jnp.zeros((n_heads, HEAD_DIM), F32)
        for s, v_ref in zip(scores, v_refs[b * ppb:(b + 1) * ppb]):
            p = jnp.exp(s - m)
            l = l + jnp.sum(p, axis=1, keepdims=True)
            acc = acc + jnp.dot(p.astype(BF16), v_ref[...].reshape(rows, HEAD_DIM).astype(BF16),
                                preferred_element_type=F32)
        blk = j * bps + b
        ksum_sc[blk] = ksum
        m_sc[blk] = jnp.broadcast_to(m, acc.shape)
        l_sc[blk] = jnp.broadcast_to(l, acc.shape)
        acc_sc[blk] = acc

    @pl.when(j == nblk // bps - 1)
    def _():
        kmean = ksum_sc[...] * (1.0 / MOBA_BLOCK)
        s_blk = jnp.sum(kmean * q[None], axis=-1, keepdims=True)
        n_iota = lax.broadcasted_iota(jnp.int32, s_blk.shape, 0)
        cnt = jnp.zeros(s_blk.shape, F32)
        for mm in range(nblk):
            s_m = s_blk[mm:mm + 1]
            cnt = cnt + jnp.where(s_m > s_blk, 1.0, jnp.where((s_m == s_blk) & (n_iota > mm), 1.0, 0.0))
        sel = cnt < float(MOBA_TOPK)
        m_b = m_sc[...]
        s_own = jnp.sum(q * kn_ref[...], axis=-1, keepdims=True) * scale
        m_tot = jnp.maximum(s_own, jnp.max(jnp.where(sel, m_b, NEG), axis=0))
        w_b = jnp.where(sel, jnp.exp(m_b - m_tot[None]), 0.0)
        w_own = jnp.exp(s_own - m_tot)
        num = w_own * vn_ref[...] + jnp.sum(w_b * acc_sc[...], axis=0)
        den = w_own + jnp.sum(w_b * l_sc[...], axis=0)
        o_ref[...] = num / den


def _moba_sample(q, k_new, v_new, cache_k, cache_v, page_table, layer):
    n, n_heads, _ = q.shape
    page = cache_k.shape[2]
    n_pages = page_table.shape[1]
    assert MOBA_BLOCK % page == 0 and (n_pages * page) % MOBA_BLOCK == 0
    ppb = MOBA_BLOCK // page
    nblk = n_pages * page // MOBA_BLOCK
    assert nblk >= MOBA_TOPK
    bps = 4 if nblk % 4 == 0 else 1
    npg = bps * ppb
    vec = pl.BlockSpec((None, n_heads, HEAD_DIM), lambda i, j, pt: (i, 0, 0))

    def page_spec(which):
        return pl.BlockSpec((None, None, page, n_heads, HEAD_DIM),
                            lambda i, j, pt: (layer, pt[i, npg * j + which], 0, 0, 0))

    pages = [page_spec(w) for w in range(npg)]
    grid_spec = pltpu.PrefetchScalarGridSpec(
        num_scalar_prefetch=1, grid=(n, nblk // bps),
        in_specs=[vec, vec, vec] + pages + pages,
        out_specs=pl.BlockSpec((None, n_heads, HEAD_DIM), lambda i, j, pt: (i, 0, 0)),
        scratch_shapes=[pltpu.VMEM((nblk, n_heads, HEAD_DIM), F32)] * 4)
    return pl.pallas_call(
        functools.partial(_moba_sample_kernel, nblk=nblk, bps=bps, ppb=ppb),
        out_shape=jax.ShapeDtypeStruct((n, n_heads, HEAD_DIM), F32),
        grid_spec=grid_spec,
        compiler_params=_cparams(("parallel", "arbitrary")),
        name="moba_sample",
    )(page_table, q, k_new, v_new, *([cache_k] * npg), *([cache_v] * npg))


def _s5_kernel(u_ref, x0_ref, bb_ref, cc_ref, ar_ref, ai_ref, d_ref, gw_ref, gb_ref,
               o_ref, st_ref, state_sc, xs_sc, y_sc, *, steps, nb, n_slabs):
    c = pl.program_id(0)
    half = S5_SLAB_GROUPS * S5_STATE
    wslab = S5_SLAB_GROUPS * S5_GROUP
    rows = steps * nb

    @pl.when(c == 0)
    def _():
        state_sc[...] = x0_ref[...]

    for gs in range(n_slabs):
        cols = slice(gs * wslab, (gs + 1) * wslab)
        ug = u_ref[:, :, cols].reshape(rows, wslab)
        xs_sc[...] = jnp.dot(ug.astype(BF16), bb_ref[gs], preferred_element_type=F32)
        ar = jnp.broadcast_to(ar_ref[gs], (nb, half))
        ai = jnp.broadcast_to(ai_ref[gs], (nb, half))

        def step(t, carry):
            xr, xi = carry
            r0 = pl.multiple_of(t * nb, nb)
            nxr = ar * xr - ai * xi + xs_sc[pl.ds(r0, nb), 0:half]
            nxi = ar * xi + ai * xr + xs_sc[pl.ds(r0, nb), half:2 * half]
            xs_sc[pl.ds(r0, nb), 0:half] = nxr
            xs_sc[pl.ds(r0, nb), half:2 * half] = nxi
            return nxr, nxi

        carry = (state_sc[gs, 0], state_sc[gs, 1])
        if steps == 1:
            xr, xi = step(0, carry)
        else:
            xr, xi = lax.fori_loop(0, steps, step, carry, unroll=4)
        state_sc[gs, 0] = xr
        state_sc[gs, 1] = xi
        y = jnp.dot(xs_sc[...].astype(BF16), cc_ref[gs], preferred_element_type=F32)
        y_sc[:, cols] = y + d_ref[:, cols] * ug

    z = jax.nn.gelu(y_sc[...])
    gate = jnp.dot(z.astype(BF16), gw_ref[...], preferred_element_type=F32) + gb_ref[...]
    o_ref[...] = (z * jax.nn.sigmoid(gate)).reshape(o_ref.shape)

    @pl.when(c == pl.num_programs(0) - 1)
    def _():
        st_ref[...] = state_sc[...]


def _s5(u_tb, x0, bb, cc, ar, ai, d, gw, gb, *, steps):
    t, nb, w = u_tb.shape
    n_slabs = bb.shape[0]
    half = S5_SLAB_GROUPS * S5_STATE
    rows = steps * nb
    const = lambda *shape: pl.BlockSpec(shape, lambda c: (0,) * len(shape))
    return pl.pallas_call(
        functools.partial(_s5_kernel, steps=steps, nb=nb, n_slabs=n_slabs),
        out_shape=(jax.ShapeDtypeStruct((t, nb, w), F32),
                   jax.ShapeDtypeStruct((n_slabs, 2, nb, half), F32)),
        grid=(t // steps,),
        in_specs=[pl.BlockSpec((steps, nb, w), lambda c: (c, 0, 0)),
                  const(n_slabs, 2, nb, half), const(*bb.shape), const(*cc.shape),
                  const(*ar.shape), const(*ai.shape), const(1, w), const(w, w), const(1, w)],
        out_specs=(pl.BlockSpec((steps, nb, w), lambda c: (c, 0, 0)), const(n_slabs, 2, nb, half)),
        scratch_shapes=[pltpu.VMEM((n_slabs, 2, nb, half), F32),
                        pltpu.VMEM((rows, 2 * half), F32),
                        pltpu.VMEM((rows, w), F32)],
        compiler_params=_cparams(("arbitrary",)),
        name="s5_scan",
    )(u_tb, x0, bb, cc, ar, ai, d, gw, gb)


def _s5_params(lam_re, lam_im, log_dt, b_ri, c_ri):
    g, p = lam_re.shape
    ns = g // S5_SLAB_GROUPS
    dt = jnp.exp(log_dt.astype(F32))[:, None]
    lr, li = lam_re.astype(F32), lam_im.astype(F32)
    mag = jnp.exp(lr * dt)
    ar, ai = mag * jnp.cos(li * dt), mag * jnp.sin(li * dt)
    d2 = lr * lr + li * li
    cr = ((ar - 1.0) * lr + ai * li) / d2
    ci = (ai * lr - (ar - 1.0) * li) / d2
    br, bi = b_ri[..., 0].astype(F32), b_ri[..., 1].astype(F32)
    bbr = cr[..., None] * br - ci[..., None] * bi
    bbi = cr[..., None] * bi + ci[..., None] * br
    eye = jnp.eye(S5_SLAB_GROUPS, dtype=F32)

    def blockdiag_in(m):
        m = m.reshape(ns, S5_SLAB_GROUPS, p, S5_GROUP)
        return jnp.einsum('sipc,ij->sicjp', m, eye).reshape(ns, S5_SLAB_GROUPS * S5_GROUP, S5_SLAB_GROUPS * p)

    def blockdiag_out(m):
        m = m.reshape(ns, S5_SLAB_GROUPS, S5_GROUP, p)
        return jnp.einsum('sicp,ij->sipjc', m, eye).reshape(ns, S5_SLAB_GROUPS * p, S5_SLAB_GROUPS * S5_GROUP)

    bb = jnp.concatenate([blockdiag_in(bbr), blockdiag_in(bbi)], axis=-1).astype(BF16)
    c_re, c_im = c_ri[..., 0].astype(F32), c_ri[..., 1].astype(F32)
    cc = jnp.concatenate([blockdiag_out(c_re), blockdiag_out(-c_im)], axis=1).astype(BF16)
    ar_s = ar.reshape(ns, 1, S5_SLAB_GROUPS * p)
    ai_s = ai.reshape(ns, 1, S5_SLAB_GROUPS * p)
    return bb, cc, ar_s, ai_s


def _s5_state_to_slabs(x0):
    n, g, p, _ = x0.shape
    ns = g // S5_SLAB_GROUPS
    return x0.reshape(n, ns, S5_SLAB_GROUPS * p, 2).transpose(1, 3, 0, 2)


def _s5_state_from_slabs(st, g):
    ns, _, n, hp = st.shape
    return st.transpose(2, 0, 3, 1).reshape(n, g, hp // S5_SLAB_GROUPS, 2)


def _outproj_even_kernel(att_ref, ga_ref, so_ref, gb_ref, w_ref, res_ref, o_ref):
    wa = att_ref.shape[1]
    a0 = (att_ref[...] * _silu(ga_ref[...])).astype(BF16)
    a1 = (so_ref[...] * _silu(gb_ref[...])).astype(BF16)
    o_ref[...] = (res_ref[...] + jnp.dot(a0, w_ref[:wa, :], preferred_element_type=F32)
                  + jnp.dot(a1, w_ref[wa:, :], preferred_element_type=F32))


def _outproj_even(att, gates, so2d, so_index, w, res, *, ga_col, gb_col, tm):
    n, wa = att.shape
    d = w.shape[1]
    gac, gbc = ga_col // wa, gb_col // wa
    so_map = (lambda i: (i, 0)) if so_index is None else so_index
    return pl.pallas_call(
        _outproj_even_kernel,
        out_shape=jax.ShapeDtypeStruct((n, d), F32),
        grid=(n // tm,),
        in_specs=[pl.BlockSpec((tm, wa), lambda i: (i, 0)),
                  pl.BlockSpec((tm, wa), lambda i: (i, gac)),
                  pl.BlockSpec((tm, wa), so_map),
                  pl.BlockSpec((tm, wa), lambda i: (i, gbc)),
                  pl.BlockSpec(w.shape, lambda i: (0, 0), pipeline_mode=pl.Buffered(1)),
                  pl.BlockSpec((tm, d), lambda i: (i, 0))],
        out_specs=pl.BlockSpec((tm, d), lambda i: (i, 0)),
        compiler_params=_cparams(("parallel",)),
        name="outproj_even",
    )(att, gates, so2d, gates, w, res)


def _matmul_res_kernel(a_ref, w_ref, res_ref, o_ref):
    o_ref[...] = res_ref[...] + jnp.dot(a_ref[...], w_ref[...], preferred_element_type=F32)


def _matmul_res(a, w, res, *, tm):
    n, k = a.shape
    d = w.shape[1]
    return pl.pallas_call(
        _matmul_res_kernel,
        out_shape=jax.ShapeDtypeStruct((n, d), F32),
        grid=(n // tm,),
        in_specs=[pl.BlockSpec((tm, k), lambda i: (i, 0)),
                  pl.BlockSpec((k, d), lambda i: (0, 0), pipeline_mode=pl.Buffered(1)),
                  pl.BlockSpec((tm, d), lambda i: (i, 0))],
        out_specs=pl.BlockSpec((tm, d), lambda i: (i, 0)),
        compiler_params=_cparams(("parallel",)),
        name="outproj_odd",
    )(a, w, res)


def _softplus(x):
    return jnp.maximum(x, 0.0) + jnp.log1p(jnp.exp(-jnp.abs(x)))


def _ssd_prompt_kernel(z_ref, x_ref, b_ref, c_ref, dt_ref, cw_ref, cb_ref, dtb_ref, alog_ref, de_ref,
                       nw_ref, e_ref, cum_ref, y_ref, st_ref, xpad_sc, act_sc, ht_sc, y_sc, *, chunk, width):
    c = pl.program_id(1)
    L = chunk
    hist = 8
    gw = width // SSD_GROUPS
    hpg = gw // SSD_HEAD_DIM
    bw = SSD_GROUPS * SSD_STATE

    @pl.when(c == 0)
    def _():
        xpad_sc[0:hist, :] = jnp.zeros((hist, xpad_sc.shape[1]), F32)
        ht_sc[...] = jnp.zeros(ht_sc.shape, F32)

    xpad_sc[hist:hist + L, 0:width] = x_ref[...]
    xpad_sc[hist:hist + L, width:width + bw] = b_ref[...]
    xpad_sc[hist:hist + L, width + bw:width + 2 * bw] = c_ref[...]
    cs = 512
    for j in range(xpad_sc.shape[1] // cs):
        sl = slice(j * cs, (j + 1) * cs)
        ext = xpad_sc[:, sl]
        conv = cw_ref[0:1, sl] * ext
        for k in range(1, SSD_CONV):
            conv = cw_ref[k:k + 1, sl] * ext + pltpu.roll(conv, 1, 0)
        act_sc[:, sl] = _silu(conv[hist:, :] + cb_ref[:, sl])
    xpad_sc[0:hist, :] = xpad_sc[L:L + hist, :]

    dt = _softplus(dt_ref[...] + dtb_ref[...])
    a = -jnp.exp(alog_ref[...])
    adt = dt * a
    a_cs_t = _dot3(adt.T, cum_ref[...])
    a_cs = a_cs_t.T
    ea = jnp.exp(a_cs)
    ds = jnp.exp(a_cs[L - 1:L, :] - a_cs)
    row_i = lax.broadcasted_iota(jnp.int32, (L, L), 0)
    col_i = lax.broadcasted_iota(jnp.int32, (L, L), 1)
    tri = row_i >= col_i
    dt_sp, ea_sp, ds_sp = _split3(dt)[:2], _split3(ea)[:2], _split3(ds)[:2]

    def spread(parts, e_g):
        return sum(jnp.dot(p, e_g, preferred_element_type=F32) for p in parts)

    for g in range(SSD_GROUPS):
        gsl = slice(g * gw, (g + 1) * gw)
        e_g = e_ref[:, gsl]
        dt_g = spread(dt_sp, e_g)
        ea_g = spread(ea_sp, e_g)
        ds_g = spread(ds_sp, e_g)
        xs_g = act_sc[:, gsl]
        xdt_g = xs_g * dt_g
        b_g = act_sc[:, width + g * SSD_STATE:width + (g + 1) * SSD_STATE].astype(BF16)
        c_g = act_sc[:, width + bw + g * SSD_STATE:width + bw + (g + 1) * SSD_STATE].astype(BF16)
        cb = lax.dot_general(c_g, b_g, (((1,), (1,)), ((), ())), preferred_element_type=F32)
        cb = jnp.where(tri, cb, 0.0)
        h_prev = ht_sc[g]
        y_off = jnp.dot(c_g, h_prev.astype(BF16), preferred_element_type=F32) * ea_g
        states_t = lax.dot_general(b_g, (xdt_g * ds_g).astype(BF16), (((0,), (0,)), ((), ())),
                                   preferred_element_type=F32)
        ht_sc[g] = h_prev * ea_g[L - 1:L, :] + states_t
        xdt16 = xdt_g.astype(BF16)
        for r in range(hpg):
            h = g * hpg + r
            hsl = slice(r * SSD_HEAD_DIM, (r + 1) * SSD_HEAD_DIM)
            diff = a_cs[:, h:h + 1] - a_cs_t[h:h + 1, :]
            lm = jnp.exp(jnp.minimum(diff, 0.0))
            yd = jnp.dot((cb * lm).astype(BF16), xdt16[:, hsl], preferred_element_type=F32)
            y_sc[:, hsl] = yd + y_off[:, hsl] + de_ref[:, g * gw + r * SSD_HEAD_DIM:g * gw + (r + 1) * SSD_HEAD_DIM] * xs_g[:, hsl]
        y = y_sc[...] * _silu(z_ref[:, gsl])
        ms = jnp.mean(y * y, axis=-1, keepdims=True)
        y_ref[:, gsl] = (y * lax.rsqrt(ms + EPS) * nw_ref[:, gsl]).astype(y_ref.dtype)

    @pl.when(c == pl.num_programs(1) - 1)
    def _():
        for g in range(SSD_GROUPS):
            st_ref[g * gw:(g + 1) * gw, :] = ht_sc[g].T


def _ssd_prompt(proj, cw, cb, dtb, alog, de, nw, e_mat, tril, *, batch, seq, width):
    L = SSD_CHUNK
    assert seq % L == 0
    nc = seq // L
    bw = SSD_GROUPS * SSD_STATE
    conv_dim = width + 2 * bw
    xcol = width // width
    bcol = (2 * width) // bw
    ccol = (2 * width + bw) // bw
    dcol = (2 * width + 2 * bw) // 128
    const = lambda *shape: pl.BlockSpec(shape, lambda b, c: (0,) * len(shape))
    return pl.pallas_call(
        functools.partial(_ssd_prompt_kernel, chunk=L, width=width),
        out_shape=(jax.ShapeDtypeStruct((batch * seq, width), BF16),
                   jax.ShapeDtypeStruct((batch, width, SSD_STATE), F32)),
        grid=(batch, nc),
        in_specs=[pl.BlockSpec((L, width), lambda b, c: (b * nc + c, 0)),
                  pl.BlockSpec((L, width), lambda b, c: (b * nc + c, xcol)),
                  pl.BlockSpec((L, bw), lambda b, c: (b * nc + c, bcol)),
                  pl.BlockSpec((L, bw), lambda b, c: (b * nc + c, ccol)),
                  pl.BlockSpec((L, 128), lambda b, c: (b * nc + c, dcol)),
                  const(SSD_CONV, conv_dim), const(1, conv_dim), const(1, 128), const(1, 128),
                  const(1, width), const(1, width), const(128, width), const(L, L)],
        out_specs=(pl.BlockSpec((L, width), lambda b, c: (b * nc + c, 0)),
                   pl.BlockSpec((None, width, SSD_STATE), lambda b, c: (b, 0, 0))),
        scratch_shapes=[pltpu.VMEM((L + 8, conv_dim), F32),
                        pltpu.VMEM((L, conv_dim), F32),
                        pltpu.VMEM((SSD_GROUPS, SSD_STATE, width // SSD_GROUPS), F32),
                        pltpu.VMEM((L, width // SSD_GROUPS), F32)],
        compiler_params=_cparams(("parallel", "arbitrary")),
        name="ssd_prompt",
    )(proj, proj, proj, proj, proj, cw, cb, dtb, alog, de, nw, e_mat, tril)


def _conv_step_kernel(x_ref, s0_ref, s1_ref, s2_ref, cw_ref, cb_ref, dt_ref, dtb_ref, alog_ref,
                      act_ref, dto_ref, dao_ref):
    conv = (cb_ref[...] + cw_ref[0:1, :] * s0_ref[...] + cw_ref[1:2, :] * s1_ref[...]
            + cw_ref[2:3, :] * s2_ref[...] + cw_ref[3:4, :] * x_ref[...])
    act_ref[...] = _silu(conv)

    @pl.when(pl.program_id(0) == 0)
    def _():
        dt = _softplus(dt_ref[...] + dtb_ref[...])
        dto_ref[...] = dt
        dao_ref[...] = jnp.exp(dt * -jnp.exp(alog_ref[...]))


def _conv_step(proj, conv_state2d, cw, cb, dtb, alog, *, width):
    n = proj.shape[0]
    conv_dim = cw.shape[1]
    cs = 512
    nj = conv_dim // cs
    x0 = width // cs
    dcol = (width + conv_dim) // 128
    return pl.pallas_call(
        _conv_step_kernel,
        out_shape=(jax.ShapeDtypeStruct((n, conv_dim), F32), jax.ShapeDtypeStruct((n, 128), F32),
                   jax.ShapeDtypeStruct((n, 128), F32)),
        grid=(nj,),
        in_specs=[pl.BlockSpec((n, cs), lambda j: (0, x0 + j)),
                  pl.BlockSpec((n, cs), lambda j: (0, j)),
                  pl.BlockSpec((n, cs), lambda j: (0, nj + j)),
                  pl.BlockSpec((n, cs), lambda j: (0, 2 * nj + j)),
                  pl.BlockSpec((SSD_CONV, cs), lambda j: (0, j)),
                  pl.BlockSpec((1, cs), lambda j: (0, j)),
                  pl.BlockSpec((n, 128), lambda j: (0, dcol)),
                  pl.BlockSpec((1, 128), lambda j: (0, 0)),
                  pl.BlockSpec((1, 128), lambda j: (0, 0))],
        out_specs=(pl.BlockSpec((n, cs), lambda j: (0, j)),
                   pl.BlockSpec((n, 128), lambda j: (0, 0)),
                   pl.BlockSpec((n, 128), lambda j: (0, 0))),
        compiler_params=_cparams(("arbitrary",)),
        name="ssd_conv_step",
    )(proj, conv_state2d, conv_state2d, conv_state2d, cw, cb, proj, dtb, alog)


def _ssd_step_kernel(dt_sm, da_sm, h0_ref, xt_ref, b_ref, c_ref, z_ref, x_ref, de_ref, nw_ref,
                     hn_ref, y_ref, *, width):
    n = pl.program_id(0)
    gw = width // SSD_GROUPS
    hpg = gw // SSD_HEAD_DIM
    nrow = xt_ref.shape[1]
    rowsel = lax.broadcasted_iota(jnp.int32, (nrow, SSD_STATE), 0) == n
    for g in range(SSD_GROUPS):
        ssl = slice(g * SSD_STATE, (g + 1) * SSD_STATE)
        gsl = slice(g * gw, (g + 1) * gw)
        rhs = jnp.where(rowsel, b_ref[:, ssl], 0.0).astype(BF16)
        outer = jnp.dot(xt_ref[gsl, :], rhs, preferred_element_type=F32)
        for r in range(hpg):
            h = g * hpg + r
            rsl = slice(g * gw + r * SSD_HEAD_DIM, g * gw + (r + 1) * SSD_HEAD_DIM)
            hn_ref[rsl, :] = (da_sm[n, h] * h0_ref[rsl, :]
                              + dt_sm[n, h] * outer[r * SSD_HEAD_DIM:(r + 1) * SSD_HEAD_DIM, :])
        c8 = jnp.broadcast_to(c_ref[:, ssl], (8, SSD_STATE)).astype(BF16)
        yg = lax.dot_general(c8, hn_ref[gsl, :].astype(BF16), (((1,), (1,)), ((), ())),
                             preferred_element_type=F32)[0:1, :]
        y = (yg + de_ref[:, gsl] * x_ref[:, gsl]) * _silu(z_ref[:, gsl])
        ms = jnp.mean(y * y, axis=-1, keepdims=True)
        y_ref[:, gsl] = y * lax.rsqrt(ms + EPS) * nw_ref[:, gsl]


def _ssd_step(dt, da, h0, xt, bm, cm, z, xs, de, nw, *, width):
    n = h0.shape[0]
    bw = SSD_GROUPS * SSD_STATE
    row = lambda w: pl.BlockSpec((None, 1, w), lambda i: (i, 0, 0))
    smem = pl.BlockSpec(memory_space=pltpu.SMEM)
    return pl.pallas_call(
        functools.partial(_ssd_step_kernel, width=width),
        out_shape=(jax.ShapeDtypeStruct(h0.shape, F32), jax.ShapeDtypeStruct((n, 1, width), F32)),
        grid=(n,),
        in_specs=[smem, smem,
                  pl.BlockSpec((None, width, SSD_STATE), lambda i: (i, 0, 0)),
                  pl.BlockSpec((width, n), lambda i: (0, 0)),
                  row(bw), row(bw), row(width), row(width),
                  pl.BlockSpec((1, width), lambda i: (0, 0)),
                  pl.BlockSpec((1, width), lambda i: (0, 0))],
        out_specs=(pl.BlockSpec((None, width, SSD_STATE), lambda i: (i, 0, 0)), row(width)),
        compiler_params=_cparams(("arbitrary",)),
        name="ssd_step",
    )(dt, da, h0, xt, bm, cm, z, xs, de, nw)


def _rope_tables(pos):
    half = HEAD_DIM // 2
    inv = ROPE_THETA ** (-jnp.arange(half, dtype=F32) / half)
    ang = pos.astype(F32)[:, None] * inv[None, :]
    cos, sin = jnp.cos(ang), jnp.sin(ang)
    return jnp.concatenate([cos, cos], axis=-1), jnp.concatenate([-sin, sin], axis=-1)


def _even_layer(yp, ys, e, norm_w_i, cache_k, cache_v, page_table, state_s5, w_in, qn, kn,
                lam_re, lam_im, log_dt, s5_b, s5_c, s5_d, glu_w, glu_b, w_out):
    batch, seq, d = yp.shape
    nsamp = ys.shape[0]
    assert ys.shape[1] == 1
    aw = d // 2
    n_heads = aw // HEAD_DIM
    n_past = page_table.shape[1] * cache_k.shape[2]
    groups = lam_re.shape[0]

    w_in16 = w_in.astype(BF16)
    w_out16 = w_out.astype(BF16)
    nw = norm_w_i.reshape(1, d)
    qn2, kn2 = qn.reshape(1, HEAD_DIM), kn.reshape(1, HEAD_DIM)
    cos_p, sin_p = _rope_tables(jnp.arange(seq, dtype=jnp.int32))
    cos_s, sin_s = _rope_tables(jnp.full((nsamp,), n_past, dtype=jnp.int32))

    xp2 = yp.reshape(batch * seq, d)
    xs2 = ys.reshape(nsamp, d)
    tn = 512
    nt = aw // tn
    routes = [(0, 0, nt, "q"), (1, nt, 2 * nt, "k"), (2, 2 * nt, 3 * nt, "plain"),
              (0, 3 * nt, 4 * nt, "plain"), (3, 4 * nt, 5 * nt, "plain"), (0, 5 * nt, 6 * nt, "plain")]
    plain_map = lambda i, c: (i, c)
    tm_p = min(1024, seq)
    nper = seq // tm_p
    n_p = batch * seq
    outs_p = [((n_p, 3 * aw), plain_map), ((n_p, aw), plain_map), ((n_p, aw), plain_map),
              ((seq, batch * aw), lambda i, c: (i % nper, (i // nper) * nt + c))]
    outs_s = [((nsamp, 3 * aw), plain_map), ((nsamp, aw), plain_map), ((nsamp, aw), plain_map),
              ((nsamp, aw), plain_map)]
    qg_p, k_p, v_p, u_p = _norm_inproj(xp2, nw, w_in16, tm=tm_p, tn=tn, routes=routes, outs=outs_p,
                                       rope=(cos_p, sin_p, qn2, kn2))
    qg_s, k_s, v_s, u_s = _norm_inproj(xs2, nw, w_in16, tm=nsamp, tn=tn, routes=routes, outs=outs_s,
                                       rope=(cos_s, sin_s, qn2, kn2))

    att_p = _moba_prompt(qg_p, k_p, v_p, batch=batch, seq=seq, n_heads=n_heads)
    heads3 = lambda a: a.reshape(nsamp, n_heads, HEAD_DIM)
    att_s = _moba_sample(heads3(qg_s[:, :aw]), heads3(k_s), heads3(v_s), cache_k, cache_v, page_table,
                         e).reshape(nsamp, aw)

    bb, cc, ar, ai = _s5_params(lam_re, lam_im, log_dt, s5_b, s5_c)
    d2 = s5_d.reshape(1, aw).astype(F32)
    gw16 = glu_w.astype(BF16)
    gb2 = glu_b.reshape(1, aw).astype(F32)
    zero_state = jnp.zeros((groups // S5_SLAB_GROUPS, 2, batch, S5_SLAB_GROUPS * S5_STATE), F32)
    so_p, st_p = _s5(u_p.reshape(seq, batch, aw), zero_state, bb, cc, ar, ai, d2, gw16, gb2, steps=64)
    so_s, st_s = _s5(u_s.reshape(1, nsamp, aw), _s5_state_to_slabs(state_s5), bb, cc, ar, ai, d2, gw16, gb2,
                     steps=1)

    tm_o = 512
    nto = seq // tm_o
    yp_new = _outproj_even(att_p, qg_p, so_p.reshape(seq, batch * aw), lambda i: (i % nto, i // nto),
                           w_out16, xp2, ga_col=aw, gb_col=2 * aw, tm=tm_o)
    ys_new = _outproj_even(att_s, qg_s, so_s.reshape(nsamp, aw), None, w_out16, xs2,
                           ga_col=aw, gb_col=2 * aw, tm=nsamp)

    heads4 = lambda a, b, t: a.reshape(b, t, n_heads, HEAD_DIM)
    outs = (heads4(k_p, batch, seq), heads4(v_p, batch, seq), heads4(k_s, nsamp, 1), heads4(v_s, nsamp, 1),
            _s5_state_from_slabs(st_p, groups), _s5_state_from_slabs(st_s, groups))
    return yp_new.reshape(batch, seq, d), ys_new.reshape(nsamp, 1, d), outs


def _odd_layer(yp, ys, norm_w_i, state_conv, state_ssd, w_in, conv_w, conv_b, dt_bias, a_log, d_skip,
               ssd_norm_w, w_out):
    batch, seq, d = yp.shape
    nsamp = ys.shape[0]
    n_heads = a_log.shape[0]
    width = n_heads * SSD_HEAD_DIM
    bw = SSD_GROUPS * SSD_STATE
    conv_dim = width + 2 * bw
    odd_in = w_in.shape[1]
    assert odd_in == width + conv_dim + n_heads and n_heads <= 128
    tn = 512
    nout = -(-(width + conv_dim + 128) // tn) * tn

    w_in16 = jnp.pad(w_in, ((0, 0), (0, nout - odd_in))).astype(BF16)
    w_out16 = w_out.astype(BF16)
    nw = norm_w_i.reshape(1, d)
    pad128 = lambda v: jnp.pad(v.astype(F32), (0, 128 - n_heads)).reshape(1, 128)
    dtb, alog = pad128(dt_bias), pad128(a_log)
    de = jnp.repeat(d_skip.astype(F32), SSD_HEAD_DIM).reshape(1, width)
    gnw = ssd_norm_w.reshape(1, width).astype(F32)
    cb = conv_b.reshape(1, conv_dim).astype(F32)
    cw = conv_w.astype(F32)
    e_mat = (jnp.arange(128)[:, None] == (jnp.arange(width)[None, :] // SSD_HEAD_DIM)).astype(BF16)
    tril = (jnp.arange(SSD_CHUNK)[:, None] <= jnp.arange(SSD_CHUNK)[None, :]).astype(BF16)

    xp2 = yp.reshape(batch * seq, d)
    xs2 = ys.reshape(nsamp, d)
    routes = [(0, 0, nout // tn, "plain")]
    plain_map = lambda i, c: (i, c)
    proj_p, = _norm_inproj(xp2, nw, w_in16, tm=min(1024, seq), tn=tn, routes=routes,
                           outs=[((batch * seq, nout), plain_map)])
    proj_s, = _norm_inproj(xs2, nw, w_in16, tm=nsamp, tn=tn, routes=routes, outs=[((nsamp, nout), plain_map)])

    yn_p, st_p = _ssd_prompt(proj_p, cw, cb, dtb, alog, de, gnw, e_mat, tril, batch=batch, seq=seq, width=width)
    yp_new = _matmul_res(yn_p, w_out16, xp2, tm=512)

    act_s, dt_s, da_s = _conv_step(proj_s, state_conv.reshape(nsamp, (SSD_CONV - 1) * conv_dim), cw, cb, dtb, alog,
                                   width=width)
    xs_s = act_s[:, :width]
    hn_s, yn_s = _ssd_step(dt_s, da_s, state_ssd.reshape(nsamp, width, SSD_STATE), xs_s.T.astype(BF16),
                           act_s[:, width:width + bw].reshape(nsamp, 1, bw),
                           act_s[:, width + bw:].reshape(nsamp, 1, bw),
                           proj_s[:, :width].reshape(nsamp, 1, width), xs_s.reshape(nsamp, 1, width),
                           de, gnw, width=width)
    ys_new = _matmul_res(yn_s.reshape(nsamp, width).astype(BF16), w_out16, xs2, tm=nsamp)

    buf_p = proj_p.reshape(batch, seq, nout)[:, seq - (SSD_CONV - 1):, width:width + conv_dim]
    buf_s = jnp.concatenate([state_conv[:, 1:, :], proj_s[:, width:width + conv_dim].reshape(nsamp, 1, conv_dim)], axis=1)
    outs = (buf_p, buf_s, st_p.reshape(batch, n_heads, SSD_HEAD_DIM, SSD_STATE),
            hn_s.reshape(nsamp, n_heads, SSD_HEAD_DIM, SSD_STATE))
    return yp_new.reshape(batch, seq, d), ys_new.reshape(nsamp, 1, d), outs


def kernel(x_prompt, x_sample, cache_k, cache_v, page_table, state_s5, state_conv, state_ssd, norm_w, w_in_even, q_norm_w, k_norm_w, s5_lambda_re, s5_lambda_im, s5_log_dt, s5_b, s5_c, s5_d, s5_glu_w, s5_glu_b, w_out_even, w_in_odd, conv_w, conv_b, ssd_dt_bias, ssd_a_log, ssd_d, ssd_norm_w, w_out_odd):
    depth = norm_w.shape[0]
    yp, ys = x_prompt, x_sample
    even_outs, odd_outs = [], []
    for i in range(depth):
        if i % 2 == 0:
            e = i // 2
            yp, ys, outs = _even_layer(yp, ys, e, norm_w[i], cache_k, cache_v, page_table, state_s5[e],
                                       w_in_even[e], q_norm_w[e], k_norm_w[e], s5_lambda_re[e], s5_lambda_im[e],
                                       s5_log_dt[e], s5_b[e], s5_c[e], s5_d[e], s5_glu_w[e], s5_glu_b[e], w_out_even[e])
            even_outs.append(outs)
        else:
            o = i // 2
            yp, ys, outs = _odd_layer(yp, ys, norm_w[i], state_conv[o], state_ssd[o], w_in_odd[o], conv_w[o],
                                      conv_b[o], ssd_dt_bias[o], ssd_a_log[o], ssd_d[o], ssd_norm_w[o], w_out_odd[o])
            odd_outs.append(outs)
    stack = lambda outs, k: jnp.stack([o[k] for o in outs])
    return (yp, ys, stack(even_outs, 0), stack(even_outs, 1), stack(even_outs, 2), stack(even_outs, 3),
            stack(even_outs, 4), stack(even_outs, 5), stack(odd_outs, 0), stack(odd_outs, 1),
            stack(odd_outs, 2), stack(odd_outs, 3))
```

```python
import functools

import jax
import jax.numpy as jnp
from jax import lax
from jax.experimental import pallas as pl
from jax.experimental.pallas import tpu as pltpu

F32 = jnp.float32
BF16 = jnp.bfloat16

HEAD_DIM = 128
MOBA_BLOCK = 256
MOBA_TOPK = 3
ROPE_THETA = 10000.0
S5_GROUP = 16
S5_STATE = 64
S5_SLAB_GROUPS = 8
SSD_HEAD_DIM = 64
SSD_GROUPS = 8
SSD_STATE = 128
SSD_CONV = 4
SSD_CHUNK = 128
EPS = 1e-6
NEG = -1e30
VMEM_LIMIT = 56 * 1024 * 1024


def _cparams(sem):
    return pltpu.CompilerParams(dimension_semantics=sem, vmem_limit_bytes=VMEM_LIMIT)


def _silu(x):
    hx = 0.5 * x
    return hx + hx * jnp.tanh(hx)


def _split3(v):
    hi = v.astype(BF16)
    r1 = v - hi.astype(F32)
    mid = r1.astype(BF16)
    lo = (r1 - mid.astype(F32)).astype(BF16)
    return hi, mid, lo


def _dot3(v, m):
    hi, mid, lo = _split3(v)
    return (jnp.dot(hi, m, preferred_element_type=F32) + jnp.dot(mid, m, preferred_element_type=F32)
            + jnp.dot(lo, m, preferred_element_type=F32))


def _norm_inproj_kernel(*refs, routes, has_rope, tn):
    n_out = 1 + max(r[0] for r in routes)
    ins, outs, h_sc = refs[:-(n_out + 1)], refs[-(n_out + 1):-1], refs[-1]
    if has_rope:
        x_ref, nw_ref, w_ref, cos_ref, sin_ref, qn_ref, kn_ref = ins
    else:
        x_ref, nw_ref, w_ref = ins
    j = pl.program_id(1)

    @pl.when(j == 0)
    def _():
        x = x_ref[...]
        ms = jnp.mean(x * x, axis=-1, keepdims=True)
        h_sc[...] = (x * lax.rsqrt(ms + EPS) * nw_ref[...]).astype(BF16)

    acc = jnp.dot(h_sc[...], w_ref[...], preferred_element_type=F32)
    for out_idx, j0, j1, kind in routes:
        o_ref = outs[out_idx]

        @pl.when((j >= j0) & (j < j1))
        def _(o_ref=o_ref, kind=kind):
            if kind == "plain":
                o_ref[...] = acc
                return
            hw = qn_ref[...] if kind == "q" else kn_ref[...]
            cos = cos_ref[...]
            sin = sin_ref[...]
            for hh in range(tn // HEAD_DIM):
                sl = slice(hh * HEAD_DIM, (hh + 1) * HEAD_DIM)
                a = acc[:, sl]
                ms = jnp.mean(a * a, axis=-1, keepdims=True)
                a = a * lax.rsqrt(ms + EPS) * hw
                o_ref[:, sl] = a * cos + pltpu.roll(a, HEAD_DIM // 2, 1) * sin


def _largest_tile(n, cap, unit):
    return max(t for t in range(unit, cap + 1, unit) if n % t == 0)


def _owned_col(j, ranges):
    total = sum(j1 - j0 for j0, j1 in ranges)
    cnt = sum(jnp.clip(j - j0, 0, j1 - j0) for j0, j1 in ranges)
    return jnp.minimum(cnt, total - 1)


def _norm_inproj(x2d, nw, w, *, tm, tn, routes, outs, rope=None, nper=1):
    n, d = x2d.shape
    nout = w.shape[1]
    grid = (n // tm, nout // tn)
    in_specs = [pl.BlockSpec((tm, d), lambda i, j: (i, 0)),
                pl.BlockSpec((1, d), lambda i, j: (0, 0)),
                pl.BlockSpec((d, tn), lambda i, j: (0, j))]
    args = [x2d, nw, w]
    if rope is not None:
        cos2, sin2, qn, kn = rope
        in_specs += [pl.BlockSpec((tm, HEAD_DIM), lambda i, j: (i % nper, 0)),
                     pl.BlockSpec((tm, HEAD_DIM), lambda i, j: (i % nper, 0)),
                     pl.BlockSpec((1, HEAD_DIM), lambda i, j: (0, 0)),
                     pl.BlockSpec((1, HEAD_DIM), lambda i, j: (0, 0))]
        args += [cos2, sin2, qn, kn]
    out_shapes, out_specs = [], []
    for k, (shape, block_map) in enumerate(outs):
        ranges = [(j0, j1) for idx, j0, j1, _ in routes if idx == k]
        out_shapes.append(jax.ShapeDtypeStruct(shape, F32))
        out_specs.append(pl.BlockSpec(
            (tm, tn), lambda i, j, ranges=ranges, block_map=block_map: block_map(i, _owned_col(j, ranges))))
    return pl.pallas_call(
        functools.partial(_norm_inproj_kernel, routes=tuple(routes), has_rope=rope is not None, tn=tn),
        out_shape=tuple(out_shapes),
        grid=grid, in_specs=in_specs,
        out_specs=tuple(out_specs),
        scratch_shapes=[pltpu.VMEM((tm, d), BF16)],
        compiler_params=_cparams(("parallel", "arbitrary")),
        name="norm_inproj_rope" if rope is not None else "norm_inproj",
    )(*args)


MOBA_HEADS_PER_STEP = 4


def _moba_prompt_kernel(q_ref, k_ref, v_ref, o_ref, kmean_sc, k16_sc, vt_sc, *, nb):
    qb = pl.program_id(2)
    blk = MOBA_BLOCK
    scale = HEAD_DIM ** -0.5
    hps = MOBA_HEADS_PER_STEP
    hcol = lambda hh: slice(hh * HEAD_DIM, (hh + 1) * HEAD_DIM)

    @pl.when(qb == 0)
    def _():
        for hh in range(hps):
            for n in range(nb):
                kb = k_ref[n * blk:(n + 1) * blk, hcol(hh)]
                kmean_sc[hh, n:n + 1, :] = jnp.mean(kb, axis=0, keepdims=True)
                k16_sc[hh, n * blk:(n + 1) * blk, :] = kb.astype(BF16)
                vt_sc[hh, :, n * blk:(n + 1) * blk] = v_ref[n * blk:(n + 1) * blk, hcol(hh)].T.astype(BF16)

    key_i = lax.broadcasted_iota(jnp.int32, (blk, blk), 0)
    qry_i = lax.broadcasted_iota(jnp.int32, (blk, blk), 1)
    causal = key_i <= qry_i
    q16s, selbs = [], []
    for hh in range(hps):
        q = q_ref[:, hcol(hh)]
        q16s.append(q.astype(BF16))
        s_blk = lax.dot_general(kmean_sc[hh], q, (((1,), (1,)), ((), ())),
                                precision=lax.Precision.HIGHEST, preferred_element_type=F32)
        n_iota = lax.broadcasted_iota(jnp.int32, s_blk.shape, 0)
        cnt = jnp.zeros(s_blk.shape, F32)
        for m in range(nb):
            s_m = s_blk[m:m + 1, :]
            beats = jnp.where(s_m > s_blk, 1.0, jnp.where((s_m == s_blk) & (n_iota > m), 1.0, 0.0))
            cnt = cnt + jnp.where(m < qb, beats, 0.0)
        sel = (n_iota < qb) & (cnt < float(MOBA_TOPK))
        selbs.append(jnp.where(sel, 0.0, NEG))

    for own in range(nb):
        @pl.when(qb == own)
        def _(own=own):
            scores = []
            for hh in range(hps):
                sh = []
                for kb in range(own + 1):
                    s = lax.dot_general(k16_sc[hh, kb * blk:(kb + 1) * blk, :], q16s[hh], (((1,), (1,)), ((), ())),
                                        preferred_element_type=F32) * scale
                    sh.append(jnp.where(causal, s, NEG) if kb == own else s + selbs[hh][kb:kb + 1, :])
                scores.append(sh)
            probs, inv_l = [], []
            for hh in range(hps):
                m = functools.reduce(jnp.maximum, [jnp.max(s, axis=0, keepdims=True) for s in scores[hh]])
                ps = [jnp.exp(s - m) for s in scores[hh]]
                l = functools.reduce(jnp.add, [jnp.sum(p, axis=0, keepdims=True) for p in ps])
                inv_l.append(1.0 / l)
                probs.append([p.astype(BF16) for p in ps])
            for hh in range(hps):
                acc = jnp.zeros((HEAD_DIM, blk), F32)
                for kb, p16 in enumerate(probs[hh]):
                    acc = acc + jnp.dot(vt_sc[hh, :, kb * blk:(kb + 1) * blk], p16,
                                        preferred_element_type=F32)
                o_ref[:, hcol(hh)] = (acc * inv_l[hh]).T


def _moba_prompt(q_arr, k_arr, v_arr, *, batch, seq, n_heads):
    nb = max(-(-seq // MOBA_BLOCK), MOBA_TOPK)
    assert seq % MOBA_BLOCK == 0 and nb * MOBA_BLOCK == seq
    hps = MOBA_HEADS_PER_STEP
    assert n_heads % hps == 0
    nqb = seq // MOBA_BLOCK
    cw = hps * HEAD_DIM
    return pl.pallas_call(
        functools.partial(_moba_prompt_kernel, nb=nb),
        out_shape=jax.ShapeDtypeStruct((batch * seq, n_heads * HEAD_DIM), F32),
        grid=(batch, n_heads // hps, nqb),
        in_specs=[pl.BlockSpec((MOBA_BLOCK, cw), lambda b, h, i: (b * nqb + i, h)),
                  pl.BlockSpec((seq, cw), lambda b, h, i: (b, h)),
                  pl.BlockSpec((seq, cw), lambda b, h, i: (b, h))],
        out_specs=pl.BlockSpec((MOBA_BLOCK, cw), lambda b, h, i: (b * nqb + i, h)),
        scratch_shapes=[pltpu.VMEM((hps, nb, HEAD_DIM), F32),
                        pltpu.VMEM((hps, seq, HEAD_DIM), BF16),
                        pltpu.VMEM((hps, HEAD_DIM, seq), BF16)],
        compiler_params=_cparams(("parallel", "parallel", "arbitrary")),
        name="moba_prompt",
    )(q_arr, k_arr, v_arr)


def _moba_sample_kernel(*refs, nblk, bps, ppb):
    npg = bps * ppb
    q_ref, kn_ref, vn_ref = refs[1:4]
    k_refs, v_refs = refs[4:4 + npg], refs[4 + npg:4 + 2 * npg]
    o_ref, ksum_sc, m_sc, l_sc, acc_sc = refs[4 + 2 * npg:]
    j = pl.program_id(1)
    scale = HEAD_DIM ** -0.5
    q = q_ref[...]
    q16 = q.astype(BF16)
    n_heads = q.shape[0]
    rows = k_refs[0].shape[0] * n_heads
    lane_head = lax.broadcasted_iota(jnp.int32, (n_heads, rows), 1) % n_heads
    mine = lane_head == lax.broadcasted_iota(jnp.int32, (n_heads, rows), 0)
    scores = []
    for b in range(bps):
        ksum = None
        for k_ref in k_refs[b * ppb:(b + 1) * ppb]:
            kp = k_ref[...]
            kpsum = jnp.sum(kp, axis=0)
            ksum = kpsum if ksum is None else ksum + kpsum
            s = lax.dot_general(q16, kp.reshape(rows, HEAD_DIM).astype(BF16), (((1,), (1,)), ((), ())),
                                preferred_element_type=F32) * scale
            scores.append(jnp.where(mine, s, NEG))
        ksum_sc[j * bps + b] = ksum
    probs = []
    for b in range(bps):
        sb = scores[b * ppb:(b + 1) * ppb]
        m = functools.reduce(jnp.maximum, [jnp.max(s, axis=1, keepdims=True) for s in sb])
        ps = [jnp.exp(s - m) for s in sb]
        l = functools.reduce(jnp.add, [jnp.sum(p, axis=1, keepdims=True) for p in ps])
        m_sc[j * bps + b] = jnp.broadcast_to(m, (n_heads, HEAD_DIM))
        l_sc[j * bps + b] = jnp.broadcast_to(l, (n_heads, HEAD_DIM))
        probs.append([p.astype(BF16) for p in ps])
    for b in range(bps):
        acc = jnp.zeros((n_heads, HEAD_DIM), F32)
        for p16, v_ref in zip(probs[b], v_refs[b * ppb:(b + 1) * ppb]):
            acc = acc + jnp.dot(p16, v_ref[...].reshape(rows, HEAD_DIM).astype(BF16),
                                preferred_element_type=F32)
        acc_sc[j * bps + b] = acc

    @pl.when(j == nblk // bps - 1)
    def _():
        kmean = ksum_sc[...] * (1.0 / MOBA_BLOCK)
        s_blk = jnp.sum(kmean * q[None], axis=-1, keepdims=True)
        n_iota = lax.broadcasted_iota(jnp.int32, s_blk.shape, 0)
        cnt = jnp.zeros(s_blk.shape, F32)
        for mm in range(nblk):
            s_m = s_blk[mm:mm + 1]
            cnt = cnt + jnp.where(s_m > s_blk, 1.0, jnp.where((s_m == s_blk) & (n_iota > mm), 1.0, 0.0))
        sel = cnt < float(MOBA_TOPK)
        m_b = m_sc[...]
        s_own = jnp.sum(q * kn_ref[...], axis=-1, keepdims=True) * scale
        m_tot = jnp.maximum(s_own, jnp.max(jnp.where(sel, m_b, NEG), axis=0))
        w_b = jnp.where(sel, jnp.exp(m_b - m_tot[None]), 0.0)
        w_own = jnp.exp(s_own - m_tot)
        num = w_own * vn_ref[...] + jnp.sum(w_b * acc_sc[...], axis=0)
        den = w_own + jnp.sum(w_b * l_sc[...], axis=0)
        o_ref[...] = num / den


def _moba_sample(q, k_new, v_new, cache_k, cache_v, page_table, layer):
    n, n_heads, _ = q.shape
    page = cache_k.shape[2]
    n_pages = page_table.shape[1]
    assert MOBA_BLOCK % page == 0 and (n_pages * page) % MOBA_BLOCK == 0
    ppb = MOBA_BLOCK // page
    nblk = n_pages * page // MOBA_BLOCK
    assert nblk >= MOBA_TOPK
    bps = 4 if nblk % 4 == 0 else 1
    npg = bps * ppb
    vec = pl.BlockSpec((None, n_heads, HEAD_DIM), lambda i, j, pt: (i, 0, 0))

    def page_spec(which):
        return pl.BlockSpec((None, None, page, n_heads, HEAD_DIM),
                            lambda i, j, pt: (layer, pt[i, npg * j + which], 0, 0, 0))

    pages = [page_spec(w) for w in range(npg)]
    grid_spec = pltpu.PrefetchScalarGridSpec(
        num_scalar_prefetch=1, grid=(n, nblk // bps),
        in_specs=[vec, vec, vec] + pages + pages,
        out_specs=pl.BlockSpec((None, n_heads, HEAD_DIM), lambda i, j, pt: (i, 0, 0)),
        scratch_shapes=[pltpu.VMEM((nblk, n_heads, HEAD_DIM), F32)] * 4)
    return pl.pallas_call(
        functools.partial(_moba_sample_kernel, nblk=nblk, bps=bps, ppb=ppb),
        out_shape=jax.ShapeDtypeStruct((n, n_heads, HEAD_DIM), F32),
        grid_spec=grid_spec,
        compiler_params=_cparams(("parallel", "arbitrary")),
        name="moba_sample",
    )(page_table, q, k_new, v_new, *([cache_k] * npg), *([cache_v] * npg))


def _s5_kernel(u_ref, x0_ref, bb_ref, cc_ref, ar_ref, ai_ref, d_ref, gw_ref, gb_ref,
               o_ref, st_ref, state_sc, xs_sc, y_sc, *, steps, nb, n_slabs):
    c = pl.program_id(0)
    half = S5_SLAB_GROUPS * S5_STATE
    wslab = S5_SLAB_GROUPS * S5_GROUP
    rows = steps * nb

    @pl.when(c == 0)
    def _():
        state_sc[...] = x0_ref[...]

    for gs in range(n_slabs):
        cols = slice(gs * wslab, (gs + 1) * wslab)
        ug = u_ref[:, :, cols].reshape(rows, wslab)
        xs_sc[...] = jnp.dot(ug.astype(BF16), bb_ref[gs], preferred_element_type=F32)
        ar = jnp.broadcast_to(ar_ref[gs], (nb, half))
        ai = jnp.broadcast_to(ai_ref[gs], (nb, half))

        def step(t, carry):
            xr, xi = carry
            r0 = pl.multiple_of(t * nb, nb)
            nxr = ar * xr - ai * xi + xs_sc[pl.ds(r0, nb), 0:half]
            nxi = ar * xi + ai * xr + xs_sc[pl.ds(r0, nb), half:2 * half]
            xs_sc[pl.ds(r0, nb), 0:half] = nxr
            xs_sc[pl.ds(r0, nb), half:2 * half] = nxi
            return nxr, nxi

        carry = (state_sc[gs, 0], state_sc[gs, 1])
        if steps == 1:
            xr, xi = step(0, carry)
        else:
            xr, xi = lax.fori_loop(0, steps, step, carry, unroll=4)
        state_sc[gs, 0] = xr
        state_sc[gs, 1] = xi
        y = jnp.dot(xs_sc[...].astype(BF16), cc_ref[gs], preferred_element_type=F32)
        y_sc[:, cols] = y + d_ref[:, cols] * ug

    z = jax.nn.gelu(y_sc[...])
    gate = jnp.dot(z.astype(BF16), gw_ref[...], preferred_element_type=F32) + gb_ref[...]
    o_ref[...] = (z * jax.nn.sigmoid(gate)).reshape(o_ref.shape)

    @pl.when(c == pl.num_programs(0) - 1)
    def _():
        st_ref[...] = state_sc[...]


def _s5(u_tb, x0, bb, cc, ar, ai, d, gw, gb, *, steps):
    t, nb, w = u_tb.shape
    n_slabs = bb.shape[0]
    half = S5_SLAB_GROUPS * S5_STATE
    rows = steps * nb
    const = lambda *shape: pl.BlockSpec(shape, lambda c: (0,) * len(shape))
    return pl.pallas_call(
        functools.partial(_s5_kernel, steps=steps, nb=nb, n_slabs=n_slabs),
        out_shape=(jax.ShapeDtypeStruct((t, nb, w), F32),
                   jax.ShapeDtypeStruct((n_slabs, 2, nb, half), F32)),
        grid=(t // steps,),
        in_specs=[pl.BlockSpec((steps, nb, w), lambda c: (c, 0, 0)),
                  const(n_slabs, 2, nb, half), const(*bb.shape), const(*cc.shape),
                  const(*ar.shape), const(*ai.shape), const(1, w), const(w, w), const(1, w)],
        out_specs=(pl.BlockSpec((steps, nb, w), lambda c: (c, 0, 0)), const(n_slabs, 2, nb, half)),
        scratch_shapes=[pltpu.VMEM((n_slabs, 2, nb, half), F32),
                        pltpu.VMEM((rows, 2 * half), F32),
                        pltpu.VMEM((rows, w), F32)],
        compiler_params=_cparams(("arbitrary",)),
        name="s5_scan",
    )(u_tb, x0, bb, cc, ar, ai, d, gw, gb)


def _s5_params(lam_re, lam_im, log_dt, b_ri, c_ri):
    g, p = lam_re.shape
    ns = g // S5_SLAB_GROUPS
    dt = jnp.exp(log_dt.astype(F32))[:, None]
    lr, li = lam_re.astype(F32), lam_im.astype(F32)
    mag = jnp.exp(lr * dt)
    ar, ai = mag * jnp.cos(li * dt), mag * jnp.sin(li * dt)
    d2 = lr * lr + li * li
    cr = ((ar - 1.0) * lr + ai * li) / d2
    ci = (ai * lr - (ar - 1.0) * li) / d2
    br, bi = b_ri[..., 0].astype(F32), b_ri[..., 1].astype(F32)
    bbr = cr[..., None] * br - ci[..., None] * bi
    bbi = cr[..., None] * bi + ci[..., None] * br
    eye = jnp.eye(S5_SLAB_GROUPS, dtype=F32)

    def blockdiag_in(m):
        m = m.reshape(ns, S5_SLAB_GROUPS, p, S5_GROUP)
        return jnp.einsum('sipc,ij->sicjp', m, eye).reshape(ns, S5_SLAB_GROUPS * S5_GROUP, S5_SLAB_GROUPS * p)

    def blockdiag_out(m):
        m = m.reshape(ns, S5_SLAB_GROUPS, S5_GROUP, p)
        return jnp.einsum('sicp,ij->sipjc', m, eye).reshape(ns, S5_SLAB_GROUPS * p, S5_SLAB_GROUPS * S5_GROUP)

    bb = jnp.concatenate([blockdiag_in(bbr), blockdiag_in(bbi)], axis=-1).astype(BF16)
    c_re, c_im = c_ri[..., 0].astype(F32), c_ri[..., 1].astype(F32)
    cc = jnp.concatenate([blockdiag_out(c_re), blockdiag_out(-c_im)], axis=1).astype(BF16)
    ar_s = ar.reshape(ns, 1, S5_SLAB_GROUPS * p)
    ai_s = ai.reshape(ns, 1, S5_SLAB_GROUPS * p)
    return bb, cc, ar_s, ai_s


def _s5_state_to_slabs(x0):
    n, g, p, _ = x0.shape
    ns = g // S5_SLAB_GROUPS
    return x0.reshape(n, ns, S5_SLAB_GROUPS * p, 2).transpose(1, 3, 0, 2)


def _s5_state_from_slabs(st, g):
    ns, _, n, hp = st.shape
    return st.transpose(2, 0, 3, 1).reshape(n, g, hp // S5_SLAB_GROUPS, 2)


def _outproj_even_kernel(att_ref, ga_ref, so_ref, gb_ref, w_ref, res_ref, o_ref):
    wa = att_ref.shape[1]
    a0 = (att_ref[...] * _silu(ga_ref[...])).astype(BF16)
    a1 = (so_ref[...] * _silu(gb_ref[...])).astype(BF16)
    o_ref[...] = (res_ref[...] + jnp.dot(a0, w_ref[:wa, :], preferred_element_type=F32)
                  + jnp.dot(a1, w_ref[wa:, :], preferred_element_type=F32))


def _outproj_even(att, gates, so2d, so_index, w, res, *, ga_col, gb_col, tm):
    n, wa = att.shape
    d = w.shape[1]
    gac, gbc = ga_col // wa, gb_col // wa
    so_map = (lambda i: (i, 0)) if so_index is None else so_index
    return pl.pallas_call(
        _outproj_even_kernel,
        out_shape=jax.ShapeDtypeStruct((n, d), F32),
        grid=(n // tm,),
        in_specs=[pl.BlockSpec((tm, wa), lambda i: (i, 0)),
                  pl.BlockSpec((tm, wa), lambda i: (i, gac)),
                  pl.BlockSpec((tm, wa), so_map),
                  pl.BlockSpec((tm, wa), lambda i: (i, gbc)),
                  pl.BlockSpec(w.shape, lambda i: (0, 0), pipeline_mode=pl.Buffered(1)),
                  pl.BlockSpec((tm, d), lambda i: (i, 0))],
        out_specs=pl.BlockSpec((tm, d), lambda i: (i, 0)),
        compiler_params=_cparams(("parallel",)),
        name="outproj_even",
    )(att, gates, so2d, gates, w, res)


def _matmul_res_kernel(a_ref, w_ref, res_ref, o_ref):
    o_ref[...] = res_ref[...] + jnp.dot(a_ref[...], w_ref[...], preferred_element_type=F32)


def _matmul_res(a, w, res, *, tm):
    n, k = a.shape
    d = w.shape[1]
    return pl.pallas_call(
        _matmul_res_kernel,
        out_shape=jax.ShapeDtypeStruct((n, d), F32),
        grid=(n // tm,),
        in_specs=[pl.BlockSpec((tm, k), lambda i: (i, 0)),
                  pl.BlockSpec((k, d), lambda i: (0, 0), pipeline_mode=pl.Buffered(1)),
                  pl.BlockSpec((tm, d), lambda i: (i, 0))],
        out_specs=pl.BlockSpec((tm, d), lambda i: (i, 0)),
        compiler_params=_cparams(("parallel",)),
        name="outproj_odd",
    )(a, w, res)


def _softplus(x):
    return jnp.maximum(x, 0.0) + jnp.log1p(jnp.exp(-jnp.abs(x)))


def _ssd_prompt_kernel(z_ref, x_ref, b_ref, c_ref, dt_ref, cw_ref, cb_ref, dtb_ref, alog_ref, de_ref,
                       nw_ref, e2_ref, cum_ref, y_ref, st_ref, xpad_sc, act_sc, ht_sc, y_sc, *, chunk, width):
    c = pl.program_id(1)
    L = chunk
    hist = 8
    gw = width // SSD_GROUPS
    hpg = gw // SSD_HEAD_DIM
    bw = SSD_GROUPS * SSD_STATE
    pw = 2 * SSD_HEAD_DIM
    assert pw == 128 and hpg % 2 == 0

    @pl.when(c == 0)
    def _():
        xpad_sc[0:hist, :] = jnp.zeros((hist, xpad_sc.shape[1]), F32)
        ht_sc[...] = jnp.zeros(ht_sc.shape, F32)

    xpad_sc[hist:hist + L, 0:width] = x_ref[...]
    xpad_sc[hist:hist + L, width:width + bw] = b_ref[...]
    xpad_sc[hist:hist + L, width + bw:width + 2 * bw] = c_ref[...]
    cs = 512
    for j in range(xpad_sc.shape[1] // cs):
        sl = slice(j * cs, (j + 1) * cs)
        ext = xpad_sc[:, sl]
        conv = cw_ref[0:1, sl] * ext
        for k in range(1, SSD_CONV):
            conv = cw_ref[k:k + 1, sl] * ext + pltpu.roll(conv, 1, 0)
        act_sc[:, sl] = _silu(conv[hist:, :] + cb_ref[:, sl])
    xpad_sc[0:hist, :] = xpad_sc[L:L + hist, :]

    dt = _softplus(dt_ref[...] + dtb_ref[...])
    a = -jnp.exp(alog_ref[...])
    adt = dt * a
    a_cs_t = _dot3(adt.T, cum_ref[...])
    a_cs = a_cs_t.T
    ea = jnp.exp(a_cs)
    ds = jnp.exp(a_cs[L - 1:L, :] - a_cs)
    row_i = lax.broadcasted_iota(jnp.int32, (L, L), 0)
    col_i = lax.broadcasted_iota(jnp.int32, (L, L), 1)
    tri = row_i >= col_i
    first_head = lax.broadcasted_iota(jnp.int32, (L, pw), 1) < SSD_HEAD_DIM

    def hi_mid(v):
        hi, mid, _ = _split3(v)
        return jnp.concatenate([hi, mid], axis=1)

    dt_p, ea_p, ds_p = hi_mid(dt), hi_mid(ea), hi_mid(ds)

    for g in range(SSD_GROUPS):
        gsl = slice(g * gw, (g + 1) * gw)
        e2_g = e2_ref[:, gsl]
        dt_g = jnp.dot(dt_p, e2_g, preferred_element_type=F32)
        ea_g = jnp.dot(ea_p, e2_g, preferred_element_type=F32)
        ds_g = jnp.dot(ds_p, e2_g, preferred_element_type=F32)
        xs_g = act_sc[:, gsl]
        xdt_g = xs_g * dt_g
        b_g = act_sc[:, width + g * SSD_STATE:width + (g + 1) * SSD_STATE].astype(BF16)
        c_g = act_sc[:, width + bw + g * SSD_STATE:width + bw + (g + 1) * SSD_STATE].astype(BF16)
        cb = lax.dot_general(c_g, b_g, (((1,), (1,)), ((), ())), preferred_element_type=F32)
        cb = jnp.where(tri, cb, 0.0)
        h_prev = ht_sc[g]
        y_off = jnp.dot(c_g, h_prev.astype(BF16), preferred_element_type=F32) * ea_g
        states_t = lax.dot_general(b_g, (xdt_g * ds_g).astype(BF16), (((0,), (0,)), ((), ())),
                                   preferred_element_type=F32)
        ht_sc[g] = h_prev * ea_g[L - 1:L, :] + states_t
        xdt16 = xdt_g.astype(BF16)
        for pr in range(hpg // 2):
            psl = slice(pr * pw, (pr + 1) * pw)
            decays = []
            for h in (g * hpg + 2 * pr, g * hpg + 2 * pr + 1):
                diff = a_cs[:, h:h + 1] - a_cs_t[h:h + 1, :]
                decays.append((cb * jnp.exp(jnp.minimum(diff, 0.0))).astype(BF16))
            xp = xdt16[:, psl]
            zero = jnp.zeros_like(xp)
            rhs = jnp.concatenate([jnp.where(first_head, xp, zero), jnp.where(first_head, zero, xp)], axis=0)
            yd = jnp.dot(jnp.concatenate(decays, axis=1), rhs, preferred_element_type=F32)
            y_sc[:, psl] = yd + y_off[:, psl] + de_ref[:, g * gw + pr * pw:g * gw + (pr + 1) * pw] * xs_g[:, psl]
        y = y_sc[...] * _silu(z_ref[:, gsl])
        ms = jnp.mean(y * y, axis=-1, keepdims=True)
        y_ref[:, gsl] = (y * lax.rsqrt(ms + EPS) * nw_ref[:, gsl]).astype(y_ref.dtype)

    @pl.when(c == pl.num_programs(1) - 1)
    def _():
        for g in range(SSD_GROUPS):
            st_ref[g * gw:(g + 1) * gw, :] = ht_sc[g].T


def _ssd_prompt(proj, cw, cb, dtb, alog, de, nw, e2_mat, tril, *, batch, seq, width):
    L = SSD_CHUNK
    assert seq % L == 0
    nc = seq // L
    bw = SSD_GROUPS * SSD_STATE
    conv_dim = width + 2 * bw
    xcol = width // width
    bcol = (2 * width) // bw
    ccol = (2 * width + bw) // bw
    dcol = (2 * width + 2 * bw) // 128
    const = lambda *shape: pl.BlockSpec(shape, lambda b, c: (0,) * len(shape))
    return pl.pallas_call(
        functools.partial(_ssd_prompt_kernel, chunk=L, width=width),
        out_shape=(jax.ShapeDtypeStruct((batch * seq, width), BF16),
                   jax.ShapeDtypeStruct((batch, width, SSD_STATE), F32)),
        grid=(batch, nc),
        in_specs=[pl.BlockSpec((L, width), lambda b, c: (b * nc + c, 0)),
                  pl.BlockSpec((L, width), lambda b, c: (b * nc + c, xcol)),
                  pl.BlockSpec((L, bw), lambda b, c: (b * nc + c, bcol)),
                  pl.BlockSpec((L, bw), lambda b, c: (b * nc + c, ccol)),
                  pl.BlockSpec((L, 128), lambda b, c: (b * nc + c, dcol)),
                  const(SSD_CONV, conv_dim), const(1, conv_dim), const(1, 128), const(1, 128),
                  const(1, width), const(1, width), const(256, width), const(L, L)],
        out_specs=(pl.BlockSpec((L, width), lambda b, c: (b * nc + c, 0)),
                   pl.BlockSpec((None, width, SSD_STATE), lambda b, c: (b, 0, 0))),
        scratch_shapes=[pltpu.VMEM((L + 8, conv_dim), F32),
                        pltpu.VMEM((L, conv_dim), F32),
                        pltpu.VMEM((SSD_GROUPS, SSD_STATE, width // SSD_GROUPS), F32),
                        pltpu.VMEM((L, width // SSD_GROUPS), F32)],
        compiler_params=_cparams(("parallel", "arbitrary")),
        name="ssd_prompt",
    )(proj, proj, proj, proj, proj, cw, cb, dtb, alog, de, nw, e2_mat, tril)


def _conv_step_kernel(x_ref, s0_ref, s1_ref, s2_ref, cw_ref, cb_ref, dt_ref, dtb_ref, alog_ref,
                      act_ref, dto_ref, dao_ref):
    conv = (cb_ref[...] + cw_ref[0:1, :] * s0_ref[...] + cw_ref[1:2, :] * s1_ref[...]
            + cw_ref[2:3, :] * s2_ref[...] + cw_ref[3:4, :] * x_ref[...])
    act_ref[...] = _silu(conv)

    @pl.when(pl.program_id(0) == 0)
    def _():
        dt = _softplus(dt_ref[...] + dtb_ref[...])
        dto_ref[...] = dt
        dao_ref[...] = jnp.exp(dt * -jnp.exp(alog_ref[...]))


def _conv_step(proj, conv_state2d, cw, cb, dtb, alog, *, width):
    n = proj.shape[0]
    conv_dim = cw.shape[1]
    cs = 512
    nj = conv_dim // cs
    x0 = width // cs
    dcol = (width + conv_dim) // 128
    return pl.pallas_call(
        _conv_step_kernel,
        out_shape=(jax.ShapeDtypeStruct((n, conv_dim), F32), jax.ShapeDtypeStruct((n, 128), F32),
                   jax.ShapeDtypeStruct((n, 128), F32)),
        grid=(nj,),
        in_specs=[pl.BlockSpec((n, cs), lambda j: (0, x0 + j)),
                  pl.BlockSpec((n, cs), lambda j: (0, j)),
                  pl.BlockSpec((n, cs), lambda j: (0, nj + j)),
                  pl.BlockSpec((n, cs), lambda j: (0, 2 * nj + j)),
                  pl.BlockSpec((SSD_CONV, cs), lambda j: (0, j)),
                  pl.BlockSpec((1, cs), lambda j: (0, j)),
                  pl.BlockSpec((n, 128), lambda j: (0, dcol)),
                  pl.BlockSpec((1, 128), lambda j: (0, 0)),
                  pl.BlockSpec((1, 128), lambda j: (0, 0))],
        out_specs=(pl.BlockSpec((n, cs), lambda j: (0, j)),
                   pl.BlockSpec((n, 128), lambda j: (0, 0)),
                   pl.BlockSpec((n, 128), lambda j: (0, 0))),
        compiler_params=_cparams(("arbitrary",)),
        name="ssd_conv_step",
    )(proj, conv_state2d, conv_state2d, conv_state2d, cw, cb, proj, dtb, alog)


def _ssd_step_kernel(dt_sm, da_sm, h0_ref, xt_ref, b_ref, c_ref, z_ref, x_ref, de_ref, nw_ref,
                     hn_ref, y_ref, *, width):
    n = pl.program_id(0)
    gw = width // SSD_GROUPS
    hpg = gw // SSD_HEAD_DIM
    nrow = xt_ref.shape[1]
    rowsel = lax.broadcasted_iota(jnp.int32, (nrow, SSD_STATE), 0) == n
    for g in range(SSD_GROUPS):
        ssl = slice(g * SSD_STATE, (g + 1) * SSD_STATE)
        gsl = slice(g * gw, (g + 1) * gw)
        rhs = jnp.where(rowsel, b_ref[:, ssl], 0.0).astype(BF16)
        outer = jnp.dot(xt_ref[gsl, :], rhs, preferred_element_type=F32)
        for r in range(hpg):
            h = g * hpg + r
            rsl = slice(g * gw + r * SSD_HEAD_DIM, g * gw + (r + 1) * SSD_HEAD_DIM)
            hn_ref[rsl, :] = (da_sm[n, h] * h0_ref[rsl, :]
                              + dt_sm[n, h] * outer[r * SSD_HEAD_DIM:(r + 1) * SSD_HEAD_DIM, :])
        c8 = jnp.broadcast_to(c_ref[:, ssl], (8, SSD_STATE)).astype(BF16)
        yg = lax.dot_general(c8, hn_ref[gsl, :].astype(BF16), (((1,), (1,)), ((), ())),
                             preferred_element_type=F32)[0:1, :]
        y = (yg + de_ref[:, gsl] * x_ref[:, gsl]) * _silu(z_ref[:, gsl])
        ms = jnp.mean(y * y, axis=-1, keepdims=True)
        y_ref[:, gsl] = y * lax.rsqrt(ms + EPS) * nw_ref[:, gsl]


def _ssd_step(dt, da, h0, xt, bm, cm, z, xs, de, nw, *, width):
    n = h0.shape[0]
    bw = SSD_GROUPS * SSD_STATE
    row = lambda w: pl.BlockSpec((None, 1, w), lambda i: (i, 0, 0))
    smem = pl.BlockSpec(memory_space=pltpu.SMEM)
    return pl.pallas_call(
        functools.partial(_ssd_step_kernel, width=width),
        out_shape=(jax.ShapeDtypeStruct(h0.shape, F32), jax.ShapeDtypeStruct((n, 1, width), F32)),
        grid=(n,),
        in_specs=[smem, smem,
                  pl.BlockSpec((None, width, SSD_STATE), lambda i: (i, 0, 0)),
                  pl.BlockSpec((width, n), lambda i: (0, 0)),
                  row(bw), row(bw), row(width), row(width),
                  pl.BlockSpec((1, width), lambda i: (0, 0)),
                  pl.BlockSpec((1, width), lambda i: (0, 0))],
        out_specs=(pl.BlockSpec((None, width, SSD_STATE), lambda i: (i, 0, 0)), row(width)),
        compiler_params=_cparams(("arbitrary",)),
        name="ssd_step",
    )(dt, da, h0, xt, bm, cm, z, xs, de, nw)


def _rope_tables(pos):
    half = HEAD_DIM // 2
    inv = ROPE_THETA ** (-jnp.arange(half, dtype=F32) / half)
    ang = pos.astype(F32)[:, None] * inv[None, :]
    cos, sin = jnp.cos(ang), jnp.sin(ang)
    return jnp.concatenate([cos, cos], axis=-1), jnp.concatenate([-sin, sin], axis=-1)


def _even_layer(yp, ys, e, norm_w_i, cache_k, cache_v, page_table, state_s5, w_in, qn, kn,
                lam_re, lam_im, log_dt, s5_b, s5_c, s5_d, glu_w, glu_b, w_out):
    batch, seq, d = yp.shape
    nsamp = ys.shape[0]
    assert ys.shape[1] == 1
    aw = d // 2
    n_heads = aw // HEAD_DIM
    n_past = page_table.shape[1] * cache_k.shape[2]
    groups = lam_re.shape[0]

    w_in16 = w_in.astype(BF16)
    w_out16 = w_out.astype(BF16)
    nw = norm_w_i.reshape(1, d)
    qn2, kn2 = qn.reshape(1, HEAD_DIM), kn.reshape(1, HEAD_DIM)
    cos_p, sin_p = _rope_tables(jnp.arange(seq, dtype=jnp.int32))
    cos_s, sin_s = _rope_tables(jnp.full((nsamp,), n_past, dtype=jnp.int32))

    xp2 = yp.reshape(batch * seq, d)
    xs2 = ys.reshape(nsamp, d)
    tn = 512
    nt = aw // tn
    routes = [(0, 0, nt, "q"), (1, nt, 2 * nt, "k"), (2, 2 * nt, 3 * nt, "plain"),
              (0, 3 * nt, 4 * nt, "plain"), (3, 4 * nt, 5 * nt, "plain"), (0, 5 * nt, 6 * nt, "plain")]
    plain_map = lambda i, c: (i, c)
    tm_p = min(1024, seq)
    nper = seq // tm_p
    n_p = batch * seq
    outs_p = [((n_p, 3 * aw), plain_map), ((n_p, aw), plain_map), ((n_p, aw), plain_map),
              ((seq, batch * aw), lambda i, c: (i % nper, (i // nper) * nt + c))]
    outs_s = [((nsamp, 3 * aw), plain_map), ((nsamp, aw), plain_map), ((nsamp, aw), plain_map),
              ((nsamp, aw), plain_map)]
    qg_p, k_p, v_p, u_p = _norm_inproj(xp2, nw, w_in16, tm=tm_p, tn=tn, routes=routes, outs=outs_p,
                                       rope=(cos_p, sin_p, qn2, kn2), nper=nper)
    qg_s, k_s, v_s, u_s = _norm_inproj(xs2, nw, w_in16, tm=nsamp, tn=tn, routes=routes, outs=outs_s,
                                       rope=(cos_s, sin_s, qn2, kn2))

    att_p = _moba_prompt(qg_p, k_p, v_p, batch=batch, seq=seq, n_heads=n_heads)
    heads3 = lambda a: a.reshape(nsamp, n_heads, HEAD_DIM)
    att_s = _moba_sample(heads3(qg_s[:, :aw]), heads3(k_s), heads3(v_s), cache_k, cache_v, page_table,
                         e).reshape(nsamp, aw)

    bb, cc, ar, ai = _s5_params(lam_re, lam_im, log_dt, s5_b, s5_c)
    d2 = s5_d.reshape(1, aw).astype(F32)
    gw16 = glu_w.astype(BF16)
    gb2 = glu_b.reshape(1, aw).astype(F32)
    zero_state = jnp.zeros((groups // S5_SLAB_GROUPS, 2, batch, S5_SLAB_GROUPS * S5_STATE), F32)
    so_p, st_p = _s5(u_p.reshape(seq, batch, aw), zero_state, bb, cc, ar, ai, d2, gw16, gb2, steps=64)
    so_s, st_s = _s5(u_s.reshape(1, nsamp, aw), _s5_state_to_slabs(state_s5), bb, cc, ar, ai, d2, gw16, gb2,
                     steps=1)

    tm_o = 512
    nto = seq // tm_o
    yp_new = _outproj_even(att_p, qg_p, so_p.reshape(seq, batch * aw), lambda i: (i % nto, i // nto),
                           w_out16, xp2, ga_col=aw, gb_col=2 * aw, tm=tm_o)
    ys_new = _outproj_even(att_s, qg_s, so_s.reshape(nsamp, aw), None, w_out16, xs2,
                           ga_col=aw, gb_col=2 * aw, tm=nsamp)

    heads4 = lambda a, b, t: a.reshape(b, t, n_heads, HEAD_DIM)
    outs = (heads4(k_p, batch, seq), heads4(v_p, batch, seq), heads4(k_s, nsamp, 1), heads4(v_s, nsamp, 1),
            _s5_state_from_slabs(st_p, groups), _s5_state_from_slabs(st_s, groups))
    return yp_new.reshape(batch, seq, d), ys_new.reshape(nsamp, 1, d), outs


def _odd_layer(yp, ys, norm_w_i, state_conv, state_ssd, w_in, conv_w, conv_b, dt_bias, a_log, d_skip,
               ssd_norm_w, w_out):
    batch, seq, d = yp.shape
    nsamp = ys.shape[0]
    n_heads = a_log.shape[0]
    width = n_heads * SSD_HEAD_DIM
    bw = SSD_GROUPS * SSD_STATE
    conv_dim = width + 2 * bw
    odd_in = w_in.shape[1]
    assert odd_in == width + conv_dim + n_heads and n_heads <= 128
    nout = -(-(width + conv_dim + 128) // 512) * 512
    tn = _largest_tile(nout, 1536, 256)

    w_in16 = jnp.pad(w_in, ((0, 0), (0, nout - odd_in))).astype(BF16)
    w_out16 = w_out.astype(BF16)
    nw = norm_w_i.reshape(1, d)
    pad128 = lambda v: jnp.pad(v.astype(F32), (0, 128 - n_heads)).reshape(1, 128)
    dtb, alog = pad128(dt_bias), pad128(a_log)
    de = jnp.repeat(d_skip.astype(F32), SSD_HEAD_DIM).reshape(1, width)
    gnw = ssd_norm_w.reshape(1, width).astype(F32)
    cb = conv_b.reshape(1, conv_dim).astype(F32)
    cw = conv_w.astype(F32)
    e_mat = (jnp.arange(128)[:, None] == (jnp.arange(width)[None, :] // SSD_HEAD_DIM)).astype(BF16)
    tril = (jnp.arange(SSD_CHUNK)[:, None] <= jnp.arange(SSD_CHUNK)[None, :]).astype(BF16)

    xp2 = yp.reshape(batch * seq, d)
    xs2 = ys.reshape(nsamp, d)
    plain_map = lambda i, c: (i, c)
    routes = [(0, 0, nout // tn, "plain")]
    proj_p, = _norm_inproj(xp2, nw, w_in16, tm=min(1024, seq), tn=tn, routes=routes,
                           outs=[((batch * seq, nout), plain_map)])
    proj_s, = _norm_inproj(xs2, nw, w_in16, tm=nsamp, tn=tn, routes=routes, outs=[((nsamp, nout), plain_map)])

    e2_mat = jnp.concatenate([e_mat, e_mat], axis=0)
    yn_p, st_p = _ssd_prompt(proj_p, cw, cb, dtb, alog, de, gnw, e2_mat, tril, batch=batch, seq=seq, width=width)
    yp_new = _matmul_res(yn_p, w_out16, xp2, tm=512)

    act_s, dt_s, da_s = _conv_step(proj_s, state_conv.reshape(nsamp, (SSD_CONV - 1) * conv_dim), cw, cb, dtb, alog,
                                   width=width)
    xs_s = act_s[:, :width]
    hn_s, yn_s = _ssd_step(dt_s, da_s, state_ssd.reshape(nsamp, width, SSD_STATE), xs_s.T.astype(BF16),
                           act_s[:, width:width + bw].reshape(nsamp, 1, bw),
                           act_s[:, width + bw:].reshape(nsamp, 1, bw),
                           proj_s[:, :width].reshape(nsamp, 1, width), xs_s.reshape(nsamp, 1, width),
                           de, gnw, width=width)
    ys_new = _matmul_res(yn_s.reshape(nsamp, width).astype(BF16), w_out16, xs2, tm=nsamp)

    buf_p = proj_p.reshape(batch, seq, nout)[:, seq - (SSD_CONV - 1):, width:width + conv_dim]
    buf_s = jnp.concatenate([state_conv[:, 1:, :], proj_s[:, width:width + conv_dim].reshape(nsamp, 1, conv_dim)], axis=1)
    outs = (buf_p, buf_s, st_p.reshape(batch, n_heads, SSD_HEAD_DIM, SSD_STATE),
            hn_s.reshape(nsamp, n_heads, SSD_HEAD_DIM, SSD_STATE))
    return yp_new.reshape(batch, seq, d), ys_new.reshape(nsamp, 1, d), outs


def kernel(x_prompt, x_sample, cache_k, cache_v, page_table, state_s5, state_conv, state_ssd, norm_w, w_in_even, q_norm_w, k_norm_w, s5_lambda_re, s5_lambda_im, s5_log_dt, s5_b, s5_c, s5_d, s5_glu_w, s5_glu_b, w_out_even, w_in_odd, conv_w, conv_b, ssd_dt_bias, ssd_a_log, ssd_d, ssd_norm_w, w_out_odd):
    depth = norm_w.shape[0]
    yp, ys = x_prompt, x_sample
    even_outs, odd_outs = [], []
    for i in range(depth):
        if i % 2 == 0:
            e = i // 2
            yp, ys, outs = _even_layer(yp, ys, e, norm_w[i], cache_k, cache_v, page_table, state_s5[e],
                                       w_in_even[e], q_norm_w[e], k_norm_w[e], s5_lambda_re[e], s5_lambda_im[e],
                                       s5_log_dt[e], s5_b[e], s5_c[e], s5_d[e], s5_glu_w[e], s5_glu_b[e], w_out_even[e])
            even_outs.append(outs)
        else:
            o = i // 2
            yp, ys, outs = _odd_layer(yp, ys, norm_w[i], state_conv[o], state_ssd[o], w_in_odd[o], conv_w[o],
                                      conv_b[o], ssd_dt_bias[o], ssd_a_log[o], ssd_d[o], ssd_norm_w[o], w_out_odd[o])
            odd_outs.append(outs)
    stack = lambda outs, k: jnp.stack([o[k] for o in outs])
    return (yp, ys, stack(even_outs, 0), stack(even_outs, 1), stack(even_outs, 2), stack(even_outs, 3),
            stack(even_outs, 4), stack(even_outs, 5), stack(odd_outs, 0), stack(odd_outs, 1),
            stack(odd_outs, 2), stack(odd_outs, 3))
```

```python
import functools

import jax
import jax.numpy as jnp
from jax import lax
from jax.experimental import pallas as pl
from jax.experimental.pallas import tpu as pltpu

F32 = jnp.float32
BF16 = jnp.bfloat16

HEAD_DIM = 128
MOBA_BLOCK = 256
MOBA_TOPK = 3
ROPE_THETA = 10000.0
S5_GROUP = 16
S5_STATE = 64
S5_SLAB_GROUPS = 8
SSD_HEAD_DIM = 64
SSD_GROUPS = 8
SSD_STATE = 128
SSD_CONV = 4
SSD_CHUNK = 128
EPS = 1e-6
NEG = -1e30
VMEM_LIMIT = 56 * 1024 * 1024


def _cparams(sem):
    return pltpu.CompilerParams(dimension_semantics=sem, vmem_limit_bytes=VMEM_LIMIT)


def _silu(x):
    hx = 0.5 * x
    return hx + hx * jnp.tanh(hx)


def _split3(v):
    hi = v.astype(BF16)
    r1 = v - hi.astype(F32)
    mid = r1.astype(BF16)
    lo = (r1 - mid.astype(F32)).astype(BF16)
    return hi, mid, lo


def _dot3(v, m):
    hi, mid, lo = _split3(v)
    return (jnp.dot(hi, m, preferred_element_type=F32) + jnp.dot(mid, m, preferred_element_type=F32)
            + jnp.dot(lo, m, preferred_element_type=F32))


def _norm_inproj_kernel(*refs, routes, has_rope, tn):
    n_out = 1 + max(r[0] for r in routes)
    ins, outs, h_sc = refs[:-(n_out + 1)], refs[-(n_out + 1):-1], refs[-1]
    if has_rope:
        x_ref, nw_ref, w_ref, cos_ref, sin_ref, qn_ref, kn_ref = ins
    else:
        x_ref, nw_ref, w_ref = ins
    j = pl.program_id(1)

    @pl.when(j == 0)
    def _():
        x = x_ref[...]
        ms = jnp.mean(x * x, axis=-1, keepdims=True)
        h_sc[...] = (x * lax.rsqrt(ms + EPS) * nw_ref[...]).astype(BF16)

    acc = jnp.dot(h_sc[...], w_ref[...], preferred_element_type=F32)
    for out_idx, j0, j1, kind in routes:
        o_ref = outs[out_idx]

        @pl.when((j >= j0) & (j < j1))
        def _(o_ref=o_ref, kind=kind):
            if kind == "plain":
                o_ref[...] = acc
                return
            hw = qn_ref[...] if kind == "q" else kn_ref[...]
            cos = cos_ref[...]
            sin = sin_ref[...]
            for hh in range(tn // HEAD_DIM):
                sl = slice(hh * HEAD_DIM, (hh + 1) * HEAD_DIM)
                a = acc[:, sl]
                ms = jnp.mean(a * a, axis=-1, keepdims=True)
                a = a * lax.rsqrt(ms + EPS) * hw
                o_ref[:, sl] = a * cos + pltpu.roll(a, HEAD_DIM // 2, 1) * sin


def _largest_tile(n, cap, unit):
    return max(t for t in range(unit, cap + 1, unit) if n % t == 0)


def _owned_col(j, ranges):
    total = sum(j1 - j0 for j0, j1 in ranges)
    cnt = sum(jnp.clip(j - j0, 0, j1 - j0) for j0, j1 in ranges)
    return jnp.minimum(cnt, total - 1)


def _norm_inproj(x2d, nw, w, *, tm, tn, routes, outs, rope=None, nper=1):
    n, d = x2d.shape
    nout = w.shape[1]
    grid = (n // tm, nout // tn)
    in_specs = [pl.BlockSpec((tm, d), lambda i, j: (i, 0)),
                pl.BlockSpec((1, d), lambda i, j: (0, 0)),
                pl.BlockSpec((d, tn), lambda i, j: (0, j))]
    args = [x2d, nw, w]
    if rope is not None:
        cos2, sin2, qn, kn = rope
        in_specs += [pl.BlockSpec((tm, HEAD_DIM), lambda i, j: (i % nper, 0)),
                     pl.BlockSpec((tm, HEAD_DIM), lambda i, j: (i % nper, 0)),
                     pl.BlockSpec((1, HEAD_DIM), lambda i, j: (0, 0)),
                     pl.BlockSpec((1, HEAD_DIM), lambda i, j: (0, 0))]
        args += [cos2, sin2, qn, kn]
    out_shapes, out_specs = [], []
    for k, (shape, block_map) in enumerate(outs):
        ranges = [(j0, j1) for idx, j0, j1, _ in routes if idx == k]
        out_shapes.append(jax.ShapeDtypeStruct(shape, F32))
        out_specs.append(pl.BlockSpec(
            (tm, tn), lambda i, j, ranges=ranges, block_map=block_map: block_map(i, _owned_col(j, ranges))))
    return pl.pallas_call(
        functools.partial(_norm_inproj_kernel, routes=tuple(routes), has_rope=rope is not None, tn=tn),
        out_shape=tuple(out_shapes),
        grid=grid, in_specs=in_specs,
        out_specs=tuple(out_specs),
        scratch_shapes=[pltpu.VMEM((tm, d), BF16)],
        compiler_params=_cparams(("parallel", "arbitrary")),
        name="norm_inproj_rope" if rope is not None else "norm_inproj",
    )(*args)


MOBA_HEADS_PER_STEP = 4


def _moba_prompt_kernel(q_ref, k_ref, v_ref, o_ref, kmean_sc, k16_sc, vt_sc, *, nb):
    qb = pl.program_id(2)
    blk = MOBA_BLOCK
    scale = HEAD_DIM ** -0.5
    hps = MOBA_HEADS_PER_STEP
    hcol = lambda hh: slice(hh * HEAD_DIM, (hh + 1) * HEAD_DIM)

    @pl.when(qb == 0)
    def _():
        for hh in range(hps):
            for n in range(nb):
                kb = k_ref[n * blk:(n + 1) * blk, hcol(hh)]
                kmean_sc[hh, n:n + 1, :] = jnp.mean(kb, axis=0, keepdims=True)
                k16_sc[hh, n * blk:(n + 1) * blk, :] = kb.astype(BF16)
                vt_sc[hh, :, n * blk:(n + 1) * blk] = v_ref[n * blk:(n + 1) * blk, hcol(hh)].T.astype(BF16)

    key_i = lax.broadcasted_iota(jnp.int32, (blk, blk), 0)
    qry_i = lax.broadcasted_iota(jnp.int32, (blk, blk), 1)
    causal = key_i <= qry_i
    q16s, selbs = [], []
    for hh in range(hps):
        q = q_ref[:, hcol(hh)]
        q16s.append(q.astype(BF16))
        s_blk = lax.dot_general(kmean_sc[hh], q, (((1,), (1,)), ((), ())),
                                precision=lax.Precision.HIGHEST, preferred_element_type=F32)
        n_iota = lax.broadcasted_iota(jnp.int32, s_blk.shape, 0)
        cnt = jnp.zeros(s_blk.shape, F32)
        for m in range(nb):
            s_m = s_blk[m:m + 1, :]
            beats = jnp.where(s_m > s_blk, 1.0, jnp.where((s_m == s_blk) & (n_iota > m), 1.0, 0.0))
            cnt = cnt + jnp.where(m < qb, beats, 0.0)
        sel = (n_iota < qb) & (cnt < float(MOBA_TOPK))
        selbs.append(jnp.where(sel, 0.0, NEG))

    for own in range(nb):
        @pl.when(qb == own)
        def _(own=own):
            scores = []
            for hh in range(hps):
                sh = []
                for kb in range(own + 1):
                    s = lax.dot_general(k16_sc[hh, kb * blk:(kb + 1) * blk, :], q16s[hh], (((1,), (1,)), ((), ())),
                                        preferred_element_type=F32) * scale
                    sh.append(jnp.where(causal, s, NEG) if kb == own else s + selbs[hh][kb:kb + 1, :])
                scores.append(sh)
            probs, inv_l = [], []
            for hh in range(hps):
                m = functools.reduce(jnp.maximum, [jnp.max(s, axis=0, keepdims=True) for s in scores[hh]])
                ps = [jnp.exp(s - m) for s in scores[hh]]
                l = functools.reduce(jnp.add, [jnp.sum(p, axis=0, keepdims=True) for p in ps])
                inv_l.append(1.0 / l)
                probs.append([p.astype(BF16) for p in ps])
            for hh in range(hps):
                acc = jnp.zeros((HEAD_DIM, blk), F32)
                for kb, p16 in enumerate(probs[hh]):
                    acc = acc + jnp.dot(vt_sc[hh, :, kb * blk:(kb + 1) * blk], p16,
                                        preferred_element_type=F32)
                o_ref[:, hcol(hh)] = (acc * inv_l[hh]).T


def _moba_prompt(q_arr, k_arr, v_arr, *, batch, seq, n_heads):
    nb = max(-(-seq // MOBA_BLOCK), MOBA_TOPK)
    assert seq % MOBA_BLOCK == 0 and nb * MOBA_BLOCK == seq
    hps = MOBA_HEADS_PER_STEP
    assert n_heads % hps == 0
    nqb = seq // MOBA_BLOCK
    cw = hps * HEAD_DIM
    return pl.pallas_call(
        functools.partial(_moba_prompt_kernel, nb=nb),
        out_shape=jax.ShapeDtypeStruct((batch * seq, n_heads * HEAD_DIM), F32),
        grid=(batch, n_heads // hps, nqb),
        in_specs=[pl.BlockSpec((MOBA_BLOCK, cw), lambda b, h, i: (b * nqb + i, h)),
                  pl.BlockSpec((seq, cw), lambda b, h, i: (b, h)),
                  pl.BlockSpec((seq, cw), lambda b, h, i: (b, h))],
        out_specs=pl.BlockSpec((MOBA_BLOCK, cw), lambda b, h, i: (b * nqb + i, h)),
        scratch_shapes=[pltpu.VMEM((hps, nb, HEAD_DIM), F32),
                        pltpu.VMEM((hps, seq, HEAD_DIM), BF16),
                        pltpu.VMEM((hps, HEAD_DIM, seq), BF16)],
        compiler_params=_cparams(("parallel", "parallel", "arbitrary")),
        name="moba_prompt",
    )(q_arr, k_arr, v_arr)


def _moba_sample_kernel(*refs, nblk, bps, ppb):
    npg = bps * ppb
    q_ref, kn_ref, vn_ref = refs[1:4]
    k_refs, v_refs = refs[4:4 + npg], refs[4 + npg:4 + 2 * npg]
    o_ref, ksum_sc, m_sc, l_sc, acc_sc = refs[4 + 2 * npg:]
    j = pl.program_id(1)
    scale = HEAD_DIM ** -0.5
    q = q_ref[...]
    q16 = q.astype(BF16)
    n_heads = q.shape[0]
    rows = k_refs[0].shape[0] * n_heads
    lane_head = lax.broadcasted_iota(jnp.int32, (n_heads, rows), 1) % n_heads
    mine = lane_head == lax.broadcasted_iota(jnp.int32, (n_heads, rows), 0)
    scores = []
    for b in range(bps):
        ksum = None
        for k_ref in k_refs[b * ppb:(b + 1) * ppb]:
            kp = k_ref[...]
            kpsum = jnp.sum(kp, axis=0)
            ksum = kpsum if ksum is None else ksum + kpsum
            s = lax.dot_general(q16, kp.reshape(rows, HEAD_DIM).astype(BF16), (((1,), (1,)), ((), ())),
                                preferred_element_type=F32) * scale
            scores.append(jnp.where(mine, s, NEG))
        ksum_sc[j * bps + b] = ksum
    probs = []
    for b in range(bps):
        sb = scores[b * ppb:(b + 1) * ppb]
        m = functools.reduce(jnp.maximum, [jnp.max(s, axis=1, keepdims=True) for s in sb])
        ps = [jnp.exp(s - m) for s in sb]
        l = functools.reduce(jnp.add, [jnp.sum(p, axis=1, keepdims=True) for p in ps])
        m_sc[j * bps + b] = jnp.broadcast_to(m, (n_heads, HEAD_DIM))
        l_sc[j * bps + b] = jnp.broadcast_to(l, (n_heads, HEAD_DIM))
        probs.append([p.astype(BF16) for p in ps])
    for b in range(bps):
        acc = jnp.zeros((n_heads, HEAD_DIM), F32)
        for p16, v_ref in zip(probs[b], v_refs[b * ppb:(b + 1) * ppb]):
            acc = acc + jnp.dot(p16, v_ref[...].reshape(rows, HEAD_DIM).astype(BF16),
                                preferred_element_type=F32)
        acc_sc[j * bps + b] = acc

    @pl.when(j == nblk // bps - 1)
    def _():
        kmean = ksum_sc[...] * (1.0 / MOBA_BLOCK)
        s_blk = jnp.sum(kmean * q[None], axis=-1, keepdims=True)
        n_iota = lax.broadcasted_iota(jnp.int32, s_blk.shape, 0)
        cnt = jnp.zeros(s_blk.shape, F32)
        for mm in range(nblk):
            s_m = s_blk[mm:mm + 1]
            cnt = cnt + jnp.where(s_m > s_blk, 1.0, jnp.where((s_m == s_blk) & (n_iota > mm), 1.0, 0.0))
        sel = cnt < float(MOBA_TOPK)
        m_b = m_sc[...]
        s_own = jnp.sum(q * kn_ref[...], axis=-1, keepdims=True) * scale
        m_tot = jnp.maximum(s_own, jnp.max(jnp.where(sel, m_b, NEG), axis=0))
        w_b = jnp.where(sel, jnp.exp(m_b - m_tot[None]), 0.0)
        w_own = jnp.exp(s_own - m_tot)
        num = w_own * vn_ref[...] + jnp.sum(w_b * acc_sc[...], axis=0)
        den = w_own + jnp.sum(w_b * l_sc[...], axis=0)
        o_ref[...] = num / den


def _moba_sample(q, k_new, v_new, cache_k, cache_v, page_table, layer):
    n, n_heads, _ = q.shape
    page = cache_k.shape[2]
    n_pages = page_table.shape[1]
    assert MOBA_BLOCK % page == 0 and (n_pages * page) % MOBA_BLOCK == 0
    ppb = MOBA_BLOCK // page
    nblk = n_pages * page // MOBA_BLOCK
    assert nblk >= MOBA_TOPK
    bps = next(b for b in (8, 4, 2, 1) if nblk % b == 0)
    npg = bps * ppb
    vec = pl.BlockSpec((None, n_heads, HEAD_DIM), lambda i, j, pt: (i, 0, 0))

    def page_spec(which):
        return pl.BlockSpec((None, None, page, n_heads, HEAD_DIM),
                            lambda i, j, pt: (layer, pt[i, npg * j + which], 0, 0, 0))

    pages = [page_spec(w) for w in range(npg)]
    grid_spec = pltpu.PrefetchScalarGridSpec(
        num_scalar_prefetch=1, grid=(n, nblk // bps),
        in_specs=[vec, vec, vec] + pages + pages,
        out_specs=pl.BlockSpec((None, n_heads, HEAD_DIM), lambda i, j, pt: (i, 0, 0)),
        scratch_shapes=[pltpu.VMEM((nblk, n_heads, HEAD_DIM), F32)] * 4)
    return pl.pallas_call(
        functools.partial(_moba_sample_kernel, nblk=nblk, bps=bps, ppb=ppb),
        out_shape=jax.ShapeDtypeStruct((n, n_heads, HEAD_DIM), F32),
        grid_spec=grid_spec,
        compiler_params=_cparams(("parallel", "arbitrary")),
        name="moba_sample",
    )(page_table, q, k_new, v_new, *([cache_k] * npg), *([cache_v] * npg))


def _s5_kernel(u_ref, x0_ref, bb_ref, cc_ref, ar_ref, ai_ref, d_ref, gw_ref, gb_ref,
               o_ref, st_ref, state_sc, xs_sc, y_sc, *, steps, nb, n_slabs):
    c = pl.program_id(0)
    half = S5_SLAB_GROUPS * S5_STATE
    wslab = S5_SLAB_GROUPS * S5_GROUP
    rows = steps * nb

    @pl.when(c == 0)
    def _():
        state_sc[...] = x0_ref[...]

    for gs in range(n_slabs):
        cols = slice(gs * wslab, (gs + 1) * wslab)
        ug = u_ref[:, :, cols].reshape(rows, wslab)
        xs_sc[...] = jnp.dot(ug.astype(BF16), bb_ref[gs], preferred_element_type=F32)
        ar = jnp.broadcast_to(ar_ref[gs], (nb, half))
        ai = jnp.broadcast_to(ai_ref[gs], (nb, half))

        def step(t, carry):
            xr, xi = carry
            r0 = pl.multiple_of(t * nb, nb)
            nxr = ar * xr - ai * xi + xs_sc[pl.ds(r0, nb), 0:half]
            nxi = ar * xi + ai * xr + xs_sc[pl.ds(r0, nb), half:2 * half]
            xs_sc[pl.ds(r0, nb), 0:half] = nxr
            xs_sc[pl.ds(r0, nb), half:2 * half] = nxi
            return nxr, nxi

        carry = (state_sc[gs, 0], state_sc[gs, 1])
        if steps == 1:
            xr, xi = step(0, carry)
        else:
            xr, xi = lax.fori_loop(0, steps, step, carry, unroll=4)
        state_sc[gs, 0] = xr
        state_sc[gs, 1] = xi
        y = jnp.dot(xs_sc[...].astype(BF16), cc_ref[gs], preferred_element_type=F32)
        y_sc[:, cols] = y + d_ref[:, cols] * ug

    z = jax.nn.gelu(y_sc[...])
    gate = jnp.dot(z.astype(BF16), gw_ref[...], preferred_element_type=F32) + gb_ref[...]
    o_ref[...] = (z * jax.nn.sigmoid(gate)).reshape(o_ref.shape)

    @pl.when(c == pl.num_programs(0) - 1)
    def _():
        st_ref[...] = state_sc[...]


def _s5(u_tb, x0, bb, cc, ar, ai, d, gw, gb, *, steps):
    t, nb, w = u_tb.shape
    n_slabs = bb.shape[0]
    half = S5_SLAB_GROUPS * S5_STATE
    rows = steps * nb
    const = lambda *shape: pl.BlockSpec(shape, lambda c: (0,) * len(shape))
    return pl.pallas_call(
        functools.partial(_s5_kernel, steps=steps, nb=nb, n_slabs=n_slabs),
        out_shape=(jax.ShapeDtypeStruct((t, nb, w), F32),
                   jax.ShapeDtypeStruct((n_slabs, 2, nb, half), F32)),
        grid=(t // steps,),
        in_specs=[pl.BlockSpec((steps, nb, w), lambda c: (c, 0, 0)),
                  const(n_slabs, 2, nb, half), const(*bb.shape), const(*cc.shape),
                  const(*ar.shape), const(*ai.shape), const(1, w), const(w, w), const(1, w)],
        out_specs=(pl.BlockSpec((steps, nb, w), lambda c: (c, 0, 0)), const(n_slabs, 2, nb, half)),
        scratch_shapes=[pltpu.VMEM((n_slabs, 2, nb, half), F32),
                        pltpu.VMEM((rows, 2 * half), F32),
                        pltpu.VMEM((rows, w), F32)],
        compiler_params=_cparams(("arbitrary",)),
        name="s5_scan",
    )(u_tb, x0, bb, cc, ar, ai, d, gw, gb)


def _s5_params(lam_re, lam_im, log_dt, b_ri, c_ri):
    g, p = lam_re.shape
    ns = g // S5_SLAB_GROUPS
    dt = jnp.exp(log_dt.astype(F32))[:, None]
    lr, li = lam_re.astype(F32), lam_im.astype(F32)
    mag = jnp.exp(lr * dt)
    ar, ai = mag * jnp.cos(li * dt), mag * jnp.sin(li * dt)
    d2 = lr * lr + li * li
    cr = ((ar - 1.0) * lr + ai * li) / d2
    ci = (ai * lr - (ar - 1.0) * li) / d2
    br, bi = b_ri[..., 0].astype(F32), b_ri[..., 1].astype(F32)
    bbr = cr[..., None] * br - ci[..., None] * bi
    bbi = cr[..., None] * bi + ci[..., None] * br
    eye = jnp.eye(S5_SLAB_GROUPS, dtype=F32)

    def blockdiag_in(m):
        m = m.reshape(ns, S5_SLAB_GROUPS, p, S5_GROUP)
        return jnp.einsum('sipc,ij->sicjp', m, eye).reshape(ns, S5_SLAB_GROUPS * S5_GROUP, S5_SLAB_GROUPS * p)

    def blockdiag_out(m):
        m = m.reshape(ns, S5_SLAB_GROUPS, S5_GROUP, p)
        return jnp.einsum('sicp,ij->sipjc', m, eye).reshape(ns, S5_SLAB_GROUPS * p, S5_SLAB_GROUPS * S5_GROUP)

    bb = jnp.concatenate([blockdiag_in(bbr), blockdiag_in(bbi)], axis=-1).astype(BF16)
    c_re, c_im = c_ri[..., 0].astype(F32), c_ri[..., 1].astype(F32)
    cc = jnp.concatenate([blockdiag_out(c_re), blockdiag_out(-c_im)], axis=1).astype(BF16)
    ar_s = ar.reshape(ns, 1, S5_SLAB_GROUPS * p)
    ai_s = ai.reshape(ns, 1, S5_SLAB_GROUPS * p)
    return bb, cc, ar_s, ai_s


def _s5_state_to_slabs(x0):
    n, g, p, _ = x0.shape
    ns = g // S5_SLAB_GROUPS
    return x0.reshape(n, ns, S5_SLAB_GROUPS * p, 2).transpose(1, 3, 0, 2)


def _s5_state_from_slabs(st, g):
    ns, _, n, hp = st.shape
    return st.transpose(2, 0, 3, 1).reshape(n, g, hp // S5_SLAB_GROUPS, 2)


def _outproj_even_kernel(att_ref, ga_ref, so_ref, gb_ref, w_ref, res_ref, o_ref):
    wa = att_ref.shape[1]
    a0 = (att_ref[...] * _silu(ga_ref[...])).astype(BF16)
    a1 = (so_ref[...] * _silu(gb_ref[...])).astype(BF16)
    o_ref[...] = (res_ref[...] + jnp.dot(a0, w_ref[:wa, :], preferred_element_type=F32)
                  + jnp.dot(a1, w_ref[wa:, :], preferred_element_type=F32))


def _outproj_even(att, gates, so2d, so_index, w, res, *, ga_col, gb_col, tm):
    n, wa = att.shape
    d = w.shape[1]
    gac, gbc = ga_col // wa, gb_col // wa
    so_map = (lambda i: (i, 0)) if so_index is None else so_index
    return pl.pallas_call(
        _outproj_even_kernel,
        out_shape=jax.ShapeDtypeStruct((n, d), F32),
        grid=(n // tm,),
        in_specs=[pl.BlockSpec((tm, wa), lambda i: (i, 0)),
                  pl.BlockSpec((tm, wa), lambda i: (i, gac)),
                  pl.BlockSpec((tm, wa), so_map),
                  pl.BlockSpec((tm, wa), lambda i: (i, gbc)),
                  pl.BlockSpec(w.shape, lambda i: (0, 0), pipeline_mode=pl.Buffered(1)),
                  pl.BlockSpec((tm, d), lambda i: (i, 0))],
        out_specs=pl.BlockSpec((tm, d), lambda i: (i, 0)),
        compiler_params=_cparams(("parallel",)),
        name="outproj_even",
    )(att, gates, so2d, gates, w, res)


def _matmul_res_kernel(a_ref, w_ref, res_ref, o_ref):
    o_ref[...] = res_ref[...] + jnp.dot(a_ref[...], w_ref[...], preferred_element_type=F32)


def _matmul_res(a, w, res, *, tm):
    n, k = a.shape
    d = w.shape[1]
    return pl.pallas_call(
        _matmul_res_kernel,
        out_shape=jax.ShapeDtypeStruct((n, d), F32),
        grid=(n // tm,),
        in_specs=[pl.BlockSpec((tm, k), lambda i: (i, 0)),
                  pl.BlockSpec((k, d), lambda i: (0, 0), pipeline_mode=pl.Buffered(1)),
                  pl.BlockSpec((tm, d), lambda i: (i, 0))],
        out_specs=pl.BlockSpec((tm, d), lambda i: (i, 0)),
        compiler_params=_cparams(("parallel",)),
        name="outproj_odd",
    )(a, w, res)


def _softplus(x):
    return jnp.maximum(x, 0.0) + jnp.log1p(jnp.exp(-jnp.abs(x)))


def _ssd_prompt_kernel(z_ref, x_ref, b_ref, c_ref, dt_ref, cw_ref, cb_ref, dtb_ref, alog_ref, de_ref,
                       nw_ref, e2_ref, cum_ref, y_ref, st_ref, xpad_sc, act_sc, ht_sc, y_sc, *, chunk, width):
    c = pl.program_id(1)
    L = chunk
    hist = 8
    gw = width // SSD_GROUPS
    hpg = gw // SSD_HEAD_DIM
    bw = SSD_GROUPS * SSD_STATE
    pw = 2 * SSD_HEAD_DIM
    assert pw == 128 and hpg % 2 == 0

    @pl.when(c == 0)
    def _():
        xpad_sc[0:hist, :] = jnp.zeros((hist, xpad_sc.shape[1]), F32)
        ht_sc[...] = jnp.zeros(ht_sc.shape, F32)

    xpad_sc[hist:hist + L, 0:width] = x_ref[...]
    xpad_sc[hist:hist + L, width:width + bw] = b_ref[...]
    xpad_sc[hist:hist + L, width + bw:width + 2 * bw] = c_ref[...]
    cs = 512
    for j in range(xpad_sc.shape[1] // cs):
        sl = slice(j * cs, (j + 1) * cs)
        ext = xpad_sc[:, sl]
        conv = cw_ref[0:1, sl] * ext
        for k in range(1, SSD_CONV):
            conv = cw_ref[k:k + 1, sl] * ext + pltpu.roll(conv, 1, 0)
        act_sc[:, sl] = _silu(conv[hist:, :] + cb_ref[:, sl])
    xpad_sc[0:hist, :] = xpad_sc[L:L + hist, :]

    dt = _softplus(dt_ref[...] + dtb_ref[...])
    a = -jnp.exp(alog_ref[...])
    adt = dt * a
    a_cs_t = _dot3(adt.T, cum_ref[...])
    a_cs = a_cs_t.T
    ea = jnp.exp(a_cs)
    ds = jnp.exp(a_cs[L - 1:L, :] - a_cs)
    row_i = lax.broadcasted_iota(jnp.int32, (L, L), 0)
    col_i = lax.broadcasted_iota(jnp.int32, (L, L), 1)
    tri = row_i >= col_i
    first_head = lax.broadcasted_iota(jnp.int32, (L, pw), 1) < SSD_HEAD_DIM

    def hi_mid(v):
        hi, mid, _ = _split3(v)
        return jnp.concatenate([hi, mid], axis=1)

    dt_p, ea_p, ds_p = hi_mid(dt), hi_mid(ea), hi_mid(ds)

    for g in range(SSD_GROUPS):
        gsl = slice(g * gw, (g + 1) * gw)
        e2_g = e2_ref[:, gsl]
        dt_g = jnp.dot(dt_p, e2_g, preferred_element_type=F32)
        ea_g = jnp.dot(ea_p, e2_g, preferred_element_type=F32)
        ds_g = jnp.dot(ds_p, e2_g, preferred_element_type=F32)
        xs_g = act_sc[:, gsl]
        xdt_g = xs_g * dt_g
        b_g = act_sc[:, width + g * SSD_STATE:width + (g + 1) * SSD_STATE].astype(BF16)
        c_g = act_sc[:, width + bw + g * SSD_STATE:width + bw + (g + 1) * SSD_STATE].astype(BF16)
        cb = lax.dot_general(c_g, b_g, (((1,), (1,)), ((), ())), preferred_element_type=F32)
        cb = jnp.where(tri, cb, 0.0)
        h_prev = ht_sc[g]
        y_off = jnp.dot(c_g, h_prev.astype(BF16), preferred_element_type=F32) * ea_g
        states_t = lax.dot_general(b_g, (xdt_g * ds_g).astype(BF16), (((0,), (0,)), ((), ())),
                                   preferred_element_type=F32)
        ht_sc[g] = h_prev * ea_g[L - 1:L, :] + states_t
        xdt16 = xdt_g.astype(BF16)
        for pr in range(hpg // 2):
            psl = slice(pr * pw, (pr + 1) * pw)
            decays = []
            for h in (g * hpg + 2 * pr, g * hpg + 2 * pr + 1):
                diff = a_cs[:, h:h + 1] - a_cs_t[h:h + 1, :]
                decays.append((cb * jnp.exp(jnp.minimum(diff, 0.0))).astype(BF16))
            xp = xdt16[:, psl]
            zero = jnp.zeros_like(xp)
            rhs = jnp.concatenate([jnp.where(first_head, xp, zero), jnp.where(first_head, zero, xp)], axis=0)
            yd = jnp.dot(jnp.concatenate(decays, axis=1), rhs, preferred_element_type=F32)
            y_sc[:, psl] = yd + y_off[:, psl] + de_ref[:, g * gw + pr * pw:g * gw + (pr + 1) * pw] * xs_g[:, psl]
        y = y_sc[...] * _silu(z_ref[:, gsl])
        ms = jnp.mean(y * y, axis=-1, keepdims=True)
        y_ref[:, gsl] = (y * lax.rsqrt(ms + EPS) * nw_ref[:, gsl]).astype(y_ref.dtype)

    @pl.when(c == pl.num_programs(1) - 1)
    def _():
        for g in range(SSD_GROUPS):
            st_ref[g * gw:(g + 1) * gw, :] = ht_sc[g].T


def _ssd_prompt(proj, cw, cb, dtb, alog, de, nw, e2_mat, tril, *, batch, seq, width):
    L = SSD_CHUNK
    assert seq % L == 0
    nc = seq // L
    bw = SSD_GROUPS * SSD_STATE
    conv_dim = width + 2 * bw
    xcol = width // width
    bcol = (2 * width) // bw
    ccol = (2 * width + bw) // bw
    dcol = (2 * width + 2 * bw) // 128
    const = lambda *shape: pl.BlockSpec(shape, lambda b, c: (0,) * len(shape))
    return pl.pallas_call(
        functools.partial(_ssd_prompt_kernel, chunk=L, width=width),
        out_shape=(jax.ShapeDtypeStruct((batch * seq, width), BF16),
                   jax.ShapeDtypeStruct((batch, width, SSD_STATE), F32)),
        grid=(batch, nc),
        in_specs=[pl.BlockSpec((L, width), lambda b, c: (b * nc + c, 0)),
                  pl.BlockSpec((L, width), lambda b, c: (b * nc + c, xcol)),
                  pl.BlockSpec((L, bw), lambda b, c: (b * nc + c, bcol)),
                  pl.BlockSpec((L, bw), lambda b, c: (b * nc + c, ccol)),
                  pl.BlockSpec((L, 128), lambda b, c: (b * nc + c, dcol)),
                  const(SSD_CONV, conv_dim), const(1, conv_dim), const(1, 128), const(1, 128),
                  const(1, width), const(1, width), const(256, width), const(L, L)],
        out_specs=(pl.BlockSpec((L, width), lambda b, c: (b * nc + c, 0)),
                   pl.BlockSpec((None, width, SSD_STATE), lambda b, c: (b, 0, 0))),
        scratch_shapes=[pltpu.VMEM((L + 8, conv_dim), F32),
                        pltpu.VMEM((L, conv_dim), F32),
                        pltpu.VMEM((SSD_GROUPS, SSD_STATE, width // SSD_GROUPS), F32),
                        pltpu.VMEM((L, width // SSD_GROUPS), F32)],
        compiler_params=_cparams(("parallel", "arbitrary")),
        name="ssd_prompt",
    )(proj, proj, proj, proj, proj, cw, cb, dtb, alog, de, nw, e2_mat, tril)


def _conv_step_kernel(x_ref, s0_ref, s1_ref, s2_ref, cw_ref, cb_ref, dt_ref, dtb_ref, alog_ref,
                      act_ref, dto_ref, dao_ref):
    conv = (cb_ref[...] + cw_ref[0:1, :] * s0_ref[...] + cw_ref[1:2, :] * s1_ref[...]
            + cw_ref[2:3, :] * s2_ref[...] + cw_ref[3:4, :] * x_ref[...])
    act_ref[...] = _silu(conv)

    @pl.when(pl.program_id(0) == 0)
    def _():
        dt = _softplus(dt_ref[...] + dtb_ref[...])
        dto_ref[...] = dt
        dao_ref[...] = jnp.exp(dt * -jnp.exp(alog_ref[...]))


def _conv_step(proj, conv_state2d, cw, cb, dtb, alog, *, width):
    n = proj.shape[0]
    conv_dim = cw.shape[1]
    cs = 512
    nj = conv_dim // cs
    x0 = width // cs
    dcol = (width + conv_dim) // 128
    return pl.pallas_call(
        _conv_step_kernel,
        out_shape=(jax.ShapeDtypeStruct((n, conv_dim), F32), jax.ShapeDtypeStruct((n, 128), F32),
                   jax.ShapeDtypeStruct((n, 128), F32)),
        grid=(nj,),
        in_specs=[pl.BlockSpec((n, cs), lambda j: (0, x0 + j)),
                  pl.BlockSpec((n, cs), lambda j: (0, j)),
                  pl.BlockSpec((n, cs), lambda j: (0, nj + j)),
                  pl.BlockSpec((n, cs), lambda j: (0, 2 * nj + j)),
                  pl.BlockSpec((SSD_CONV, cs), lambda j: (0, j)),
                  pl.BlockSpec((1, cs), lambda j: (0, j)),
                  pl.BlockSpec((n, 128), lambda j: (0, dcol)),
                  pl.BlockSpec((1, 128), lambda j: (0, 0)),
                  pl.BlockSpec((1, 128), lambda j: (0, 0))],
        out_specs=(pl.BlockSpec((n, cs), lambda j: (0, j)),
                   pl.BlockSpec((n, 128), lambda j: (0, 0)),
                   pl.BlockSpec((n, 128), lambda j: (0, 0))),
        compiler_params=_cparams(("arbitrary",)),
        name="ssd_conv_step",
    )(proj, conv_state2d, conv_state2d, conv_state2d, cw, cb, proj, dtb, alog)


def _ssd_step_kernel(dt_sm, da_sm, h0_ref, xt_ref, b_ref, c_ref, z_ref, x_ref, de_ref, nw_ref,
                     hn_ref, y_ref, *, width, per_step):
    gw = width // SSD_GROUPS
    hpg = gw // SSD_HEAD_DIM
    nrow = xt_ref.shape[1]
    row_i = lax.broadcasted_iota(jnp.int32, (nrow, SSD_STATE), 0)
    for s in range(per_step):
        n = pl.program_id(0) * per_step + s
        rowsel = row_i == n
        for g in range(SSD_GROUPS):
            ssl = slice(g * SSD_STATE, (g + 1) * SSD_STATE)
            gsl = slice(g * gw, (g + 1) * gw)
            rhs = jnp.where(rowsel, b_ref[s, :, ssl], 0.0).astype(BF16)
            outer = jnp.dot(xt_ref[gsl, :], rhs, preferred_element_type=F32)
            for r in range(hpg):
                h = g * hpg + r
                rsl = slice(g * gw + r * SSD_HEAD_DIM, g * gw + (r + 1) * SSD_HEAD_DIM)
                hn_ref[s, rsl, :] = (da_sm[n, h] * h0_ref[s, rsl, :]
                                     + dt_sm[n, h] * outer[r * SSD_HEAD_DIM:(r + 1) * SSD_HEAD_DIM, :])
            c8 = jnp.broadcast_to(c_ref[s, :, ssl], (8, SSD_STATE)).astype(BF16)
            yg = lax.dot_general(c8, hn_ref[s, gsl, :].astype(BF16), (((1,), (1,)), ((), ())),
                                 preferred_element_type=F32)[0:1, :]
            y = (yg + de_ref[:, gsl] * x_ref[s, :, gsl]) * _silu(z_ref[s, :, gsl])
            ms = jnp.mean(y * y, axis=-1, keepdims=True)
            y_ref[s, :, gsl] = y * lax.rsqrt(ms + EPS) * nw_ref[:, gsl]


def _ssd_step(dt, da, h0, xt, bm, cm, z, xs, de, nw, *, width):
    n = h0.shape[0]
    bw = SSD_GROUPS * SSD_STATE
    per_step = 2 if n % 2 == 0 else 1
    row = lambda w: pl.BlockSpec((per_step, 1, w), lambda i: (i, 0, 0))
    state = pl.BlockSpec((per_step, width, SSD_STATE), lambda i: (i, 0, 0))
    smem = pl.BlockSpec(memory_space=pltpu.SMEM)
    return pl.pallas_call(
        functools.partial(_ssd_step_kernel, width=width, per_step=per_step),
        out_shape=(jax.ShapeDtypeStruct(h0.shape, F32), jax.ShapeDtypeStruct((n, 1, width), F32)),
        grid=(n // per_step,),
        in_specs=[smem, smem, state,
                  pl.BlockSpec((width, n), lambda i: (0, 0)),
                  row(bw), row(bw), row(width), row(width),
                  pl.BlockSpec((1, width), lambda i: (0, 0)),
                  pl.BlockSpec((1, width), lambda i: (0, 0))],
        out_specs=(state, row(width)),
        compiler_params=_cparams(("arbitrary",)),
        name="ssd_step",
    )(dt, da, h0, xt, bm, cm, z, xs, de, nw)


def _rope_tables(pos):
    half = HEAD_DIM // 2
    inv = ROPE_THETA ** (-jnp.arange(half, dtype=F32) / half)
    ang = pos.astype(F32)[:, None] * inv[None, :]
    cos, sin = jnp.cos(ang), jnp.sin(ang)
    return jnp.concatenate([cos, cos], axis=-1), jnp.concatenate([-sin, sin], axis=-1)


def _even_layer(yp, ys, e, norm_w_i, cache_k, cache_v, page_table, state_s5, w_in, qn, kn,
                lam_re, lam_im, log_dt, s5_b, s5_c, s5_d, glu_w, glu_b, w_out):
    batch, seq, d = yp.shape
    nsamp = ys.shape[0]
    assert ys.shape[1] == 1
    aw = d // 2
    n_heads = aw // HEAD_DIM
    n_past = page_table.shape[1] * cache_k.shape[2]
    groups = lam_re.shape[0]

    w_in16 = w_in.astype(BF16)
    w_out16 = w_out.astype(BF16)
    nw = norm_w_i.reshape(1, d)
    qn2, kn2 = qn.reshape(1, HEAD_DIM), kn.reshape(1, HEAD_DIM)
    cos_p, sin_p = _rope_tables(jnp.arange(seq, dtype=jnp.int32))
    cos_s, sin_s = _rope_tables(jnp.full((nsamp,), n_past, dtype=jnp.int32))

    xp2 = yp.reshape(batch * seq, d)
    xs2 = ys.reshape(nsamp, d)
    tn = 512
    nt = aw // tn
    routes = [(0, 0, nt, "q"), (1, nt, 2 * nt, "k"), (2, 2 * nt, 3 * nt, "plain"),
              (0, 3 * nt, 4 * nt, "plain"), (3, 4 * nt, 5 * nt, "plain"), (0, 5 * nt, 6 * nt, "plain")]
    plain_map = lambda i, c: (i, c)
    tm_p = min(1024, seq)
    nper = seq // tm_p
    n_p = batch * seq
    outs_p = [((n_p, 3 * aw), plain_map), ((n_p, aw), plain_map), ((n_p, aw), plain_map),
              ((seq, batch * aw), lambda i, c: (i % nper, (i // nper) * nt + c))]
    outs_s = [((nsamp, 3 * aw), plain_map), ((nsamp, aw), plain_map), ((nsamp, aw), plain_map),
              ((nsamp, aw), plain_map)]
    qg_p, k_p, v_p, u_p = _norm_inproj(xp2, nw, w_in16, tm=tm_p, tn=tn, routes=routes, outs=outs_p,
                                       rope=(cos_p, sin_p, qn2, kn2), nper=nper)
    qg_s, k_s, v_s, u_s = _norm_inproj(xs2, nw, w_in16, tm=nsamp, tn=tn, routes=routes, outs=outs_s,
                                       rope=(cos_s, sin_s, qn2, kn2))

    att_p = _moba_prompt(qg_p, k_p, v_p, batch=batch, seq=seq, n_heads=n_heads)
    heads3 = lambda a: a.reshape(nsamp, n_heads, HEAD_DIM)
    att_s = _moba_sample(heads3(qg_s[:, :aw]), heads3(k_s), heads3(v_s), cache_k, cache_v, page_table,
                         e).reshape(nsamp, aw)

    bb, cc, ar, ai = _s5_params(lam_re, lam_im, log_dt, s5_b, s5_c)
    d2 = s5_d.reshape(1, aw).astype(F32)
    gw16 = glu_w.astype(BF16)
    gb2 = glu_b.reshape(1, aw).astype(F32)
    zero_state = jnp.zeros((groups // S5_SLAB_GROUPS, 2, batch, S5_SLAB_GROUPS * S5_STATE), F32)
    so_p, st_p = _s5(u_p.reshape(seq, batch, aw), zero_state, bb, cc, ar, ai, d2, gw16, gb2, steps=64)
    so_s, st_s = _s5(u_s.reshape(1, nsamp, aw), _s5_state_to_slabs(state_s5), bb, cc, ar, ai, d2, gw16, gb2,
                     steps=1)

    tm_o = 512
    nto = seq // tm_o
    yp_new = _outproj_even(att_p, qg_p, so_p.reshape(seq, batch * aw), lambda i: (i % nto, i // nto),
                           w_out16, xp2, ga_col=aw, gb_col=2 * aw, tm=tm_o)
    ys_new = _outproj_even(att_s, qg_s, so_s.reshape(nsamp, aw), None, w_out16, xs2,
                           ga_col=aw, gb_col=2 * aw, tm=nsamp)

    heads4 = lambda a, b, t: a.reshape(b, t, n_heads, HEAD_DIM)
    outs = (heads4(k_p, batch, seq), heads4(v_p, batch, seq), heads4(k_s, nsamp, 1), heads4(v_s, nsamp, 1),
            _s5_state_from_slabs(st_p, groups), _s5_state_from_slabs(st_s, groups))
    return yp_new.reshape(batch, seq, d), ys_new.reshape(nsamp, 1, d), outs


def _odd_layer(yp, ys, norm_w_i, state_conv, state_ssd, w_in, conv_w, conv_b, dt_bias, a_log, d_skip,
               ssd_norm_w, w_out):
    batch, seq, d = yp.shape
    nsamp = ys.shape[0]
    n_heads = a_log.shape[0]
    width = n_heads * SSD_HEAD_DIM
    bw = SSD_GROUPS * SSD_STATE
    conv_dim = width + 2 * bw
    odd_in = w_in.shape[1]
    assert odd_in == width + conv_dim + n_heads and n_heads <= 128
    nout = -(-(width + conv_dim + 128) // 512) * 512
    tn = _largest_tile(nout, 1536, 256)

    w_in16 = jnp.pad(w_in.astype(BF16), ((0, 0), (0, nout - odd_in)))
    w_out16 = w_out.astype(BF16)
    nw = norm_w_i.reshape(1, d)
    pad128 = lambda v: jnp.pad(v.astype(F32), (0, 128 - n_heads)).reshape(1, 128)
    dtb, alog = pad128(dt_bias), pad128(a_log)
    de = jnp.repeat(d_skip.astype(F32), SSD_HEAD_DIM).reshape(1, width)
    gnw = ssd_norm_w.reshape(1, width).astype(F32)
    cb = conv_b.reshape(1, conv_dim).astype(F32)
    cw = conv_w.astype(F32)
    e_mat = (jnp.arange(128)[:, None] == (jnp.arange(width)[None, :] // SSD_HEAD_DIM)).astype(BF16)
    tril = (jnp.arange(SSD_CHUNK)[:, None] <= jnp.arange(SSD_CHUNK)[None, :]).astype(BF16)

    xp2 = yp.reshape(batch * seq, d)
    xs2 = ys.reshape(nsamp, d)
    plain_map = lambda i, c: (i, c)
    routes = [(0, 0, nout // tn, "plain")]
    proj_p, = _norm_inproj(xp2, nw, w_in16, tm=min(1024, seq), tn=tn, routes=routes,
                           outs=[((batch * seq, nout), plain_map)])
    proj_s, = _norm_inproj(xs2, nw, w_in16, tm=nsamp, tn=tn, routes=routes, outs=[((nsamp, nout), plain_map)])

    e2_mat = jnp.concatenate([e_mat, e_mat], axis=0)
    yn_p, st_p = _ssd_prompt(proj_p, cw, cb, dtb, alog, de, gnw, e2_mat, tril, batch=batch, seq=seq, width=width)
    yp_new = _matmul_res(yn_p, w_out16, xp2, tm=512)

    act_s, dt_s, da_s = _conv_step(proj_s, state_conv.reshape(nsamp, (SSD_CONV - 1) * conv_dim), cw, cb, dtb, alog,
                                   width=width)
    xs_s = act_s[:, :width]
    hn_s, yn_s = _ssd_step(dt_s, da_s, state_ssd.reshape(nsamp, width, SSD_STATE), xs_s.T.astype(BF16),
                           act_s[:, width:width + bw].reshape(nsamp, 1, bw),
                           act_s[:, width + bw:].reshape(nsamp, 1, bw),
                           proj_s[:, :width].reshape(nsamp, 1, width), xs_s.reshape(nsamp, 1, width),
                           de, gnw, width=width)
    ys_new = _matmul_res(yn_s.reshape(nsamp, width).astype(BF16), w_out16, xs2, tm=nsamp)

    buf_p = proj_p.reshape(batch, seq, nout)[:, seq - (SSD_CONV - 1):, width:width + conv_dim]
    buf_s = jnp.concatenate([state_conv[:, 1:, :], proj_s[:, width:width + conv_dim].reshape(nsamp, 1, conv_dim)], axis=1)
    outs = (buf_p, buf_s, st_p.reshape(batch, n_heads, SSD_HEAD_DIM, SSD_STATE),
            hn_s.reshape(nsamp, n_heads, SSD_HEAD_DIM, SSD_STATE))
    return yp_new.reshape(batch, seq, d), ys_new.reshape(nsamp, 1, d), outs


def kernel(x_prompt, x_sample, cache_k, cache_v, page_table, state_s5, state_conv, state_ssd, norm_w, w_in_even, q_norm_w, k_norm_w, s5_lambda_re, s5_lambda_im, s5_log_dt, s5_b, s5_c, s5_d, s5_glu_w, s5_glu_b, w_out_even, w_in_odd, conv_w, conv_b, ssd_dt_bias, ssd_a_log, ssd_d, ssd_norm_w, w_out_odd):
    depth = norm_w.shape[0]
    yp, ys = x_prompt, x_sample
    even_outs, odd_outs = [], []
    for i in range(depth):
        if i % 2 == 0:
            e = i // 2
            yp, ys, outs = _even_layer(yp, ys, e, norm_w[i], cache_k, cache_v, page_table, state_s5[e],
                                       w_in_even[e], q_norm_w[e], k_norm_w[e], s5_lambda_re[e], s5_lambda_im[e],
                                       s5_log_dt[e], s5_b[e], s5_c[e], s5_d[e], s5_glu_w[e], s5_glu_b[e], w_out_even[e])
            even_outs.append(outs)
        else:
            o = i // 2
            yp, ys, outs = _odd_layer(yp, ys, norm_w[i], state_conv[o], state_ssd[o], w_in_odd[o], conv_w[o],
                                      conv_b[o], ssd_dt_bias[o], ssd_a_log[o], ssd_d[o], ssd_norm_w[o], w_out_odd[o])
            odd_outs.append(outs)
    stack = lambda outs, k: jnp.stack([o[k] for o in outs])
    return (yp, ys, stack(even_outs, 0), stack(even_outs, 1), stack(even_outs, 2), stack(even_outs, 3),
            stack(even_outs, 4), stack(even_outs, 5), stack(odd_outs, 0), stack(odd_outs, 1),
            stack(odd_outs, 2), stack(odd_outs, 3))
```

```python
import functools

import jax
import jax.numpy as jnp
from jax import lax
from jax.experimental import pallas as pl
from jax.experimental.pallas import tpu as pltpu

F32 = jnp.float32
BF16 = jnp.bfloat16

HEAD_DIM = 128
MOBA_BLOCK = 256
MOBA_TOPK = 3
ROPE_THETA = 10000.0
S5_GROUP = 16
S5_STATE = 64
S5_SLAB_GROUPS = 8
SSD_HEAD_DIM = 64
SSD_GROUPS = 8
SSD_STATE = 128
SSD_CONV = 4
SSD_CHUNK = 128
EPS = 1e-6
NEG = -1e30
VMEM_LIMIT = 58 * 1024 * 1024


def _cparams(sem):
    return pltpu.CompilerParams(dimension_semantics=sem, vmem_limit_bytes=VMEM_LIMIT)


def _silu(x):
    hx = 0.5 * x
    return hx + hx * jnp.tanh(hx)


def _split3(v):
    hi = v.astype(BF16)
    r1 = v - hi.astype(F32)
    mid = r1.astype(BF16)
    lo = (r1 - mid.astype(F32)).astype(BF16)
    return hi, mid, lo


def _dot3(v, m):
    hi, mid, lo = _split3(v)
    return (jnp.dot(hi, m, preferred_element_type=F32) + jnp.dot(mid, m, preferred_element_type=F32)
            + jnp.dot(lo, m, preferred_element_type=F32))


def _norm_inproj_kernel(*refs, routes, has_rope, has_side, tn):
    n_out = 1 + max(r[0] for r in routes) + (1 if has_side else 0)
    ins, outs, h_sc = refs[:-(n_out + 1)], refs[-(n_out + 1):-1], refs[-1]
    if has_side:
        ins, ws_ref, side_ref = ins[:-1], ins[-1], outs[-1]
    if has_rope:
        x_ref, nw_ref, w_ref, cos_ref, sin_ref, qn_ref, kn_ref = ins
    else:
        x_ref, nw_ref, w_ref = ins
    j = pl.program_id(1)

    @pl.when(j == 0)
    def _():
        x = x_ref[...]
        ms = jnp.mean(x * x, axis=-1, keepdims=True)
        h_sc[...] = (x * lax.rsqrt(ms + EPS) * nw_ref[...]).astype(BF16)
        if has_side:
            side_ref[...] = jnp.dot(h_sc[...], ws_ref[...], preferred_element_type=F32)

    acc = jnp.dot(h_sc[...], w_ref[...], preferred_element_type=F32)
    for out_idx, j0, j1, kind in routes:
        o_ref = outs[out_idx]

        @pl.when((j >= j0) & (j < j1))
        def _(o_ref=o_ref, kind=kind):
            if kind == "plain":
                o_ref[...] = acc
                return
            hw = qn_ref[...] if kind == "q" else kn_ref[...]
            cos = cos_ref[...]
            sin = sin_ref[...]
            for hh in range(tn // HEAD_DIM):
                sl = slice(hh * HEAD_DIM, (hh + 1) * HEAD_DIM)
                a = acc[:, sl]
                ms = jnp.mean(a * a, axis=-1, keepdims=True)
                a = a * lax.rsqrt(ms + EPS) * hw
                o_ref[:, sl] = a * cos + pltpu.roll(a, HEAD_DIM // 2, 1) * sin


def _largest_tile(n, cap, unit):
    return max(t for t in range(unit, cap + 1, unit) if n % t == 0)


def _owned_col(j, ranges):
    total = sum(j1 - j0 for j0, j1 in ranges)
    cnt = sum(jnp.clip(j - j0, 0, j1 - j0) for j0, j1 in ranges)
    return jnp.minimum(cnt, total - 1)


def _norm_inproj(x2d, nw, w, *, tm, tn, routes, outs, rope=None, nper=1, side_w=None):
    n, d = x2d.shape
    nout = w.shape[1]
    grid = (n // tm, nout // tn)
    in_specs = [pl.BlockSpec((tm, d), lambda i, j: (i, 0)),
                pl.BlockSpec((1, d), lambda i, j: (0, 0)),
                pl.BlockSpec((d, tn), lambda i, j: (0, j))]
    args = [x2d, nw, w]
    if rope is not None:
        cos2, sin2, qn, kn = rope
        in_specs += [pl.BlockSpec((tm, HEAD_DIM), lambda i, j: (i % nper, 0), pipeline_mode=pl.Buffered(1)),
                     pl.BlockSpec((tm, HEAD_DIM), lambda i, j: (i % nper, 0), pipeline_mode=pl.Buffered(1)),
                     pl.BlockSpec((1, HEAD_DIM), lambda i, j: (0, 0)),
                     pl.BlockSpec((1, HEAD_DIM), lambda i, j: (0, 0))]
        args += [cos2, sin2, qn, kn]
    out_shapes, out_specs = [], []
    for k, (shape, block_map) in enumerate(outs):
        ranges = [(j0, j1) for idx, j0, j1, _ in routes if idx == k]
        out_shapes.append(jax.ShapeDtypeStruct(shape, F32))
        single = sum(j1 - j0 for j0, j1 in ranges) == 1 and grid[1] > 2
        out_specs.append(pl.BlockSpec(
            (tm, tn), lambda i, j, ranges=ranges, block_map=block_map: block_map(i, _owned_col(j, ranges)),
            **({"pipeline_mode": pl.Buffered(1)} if single else {})))
    if side_w is not None:
        in_specs.append(pl.BlockSpec(side_w.shape, lambda i, j: (0, 0)))
        args.append(side_w)
        out_shapes.append(jax.ShapeDtypeStruct((n, side_w.shape[1]), F32))
        out_specs.append(pl.BlockSpec((tm, side_w.shape[1]), lambda i, j: (i, 0)))
    return pl.pallas_call(
        functools.partial(_norm_inproj_kernel, routes=tuple(routes), has_rope=rope is not None,
                          has_side=side_w is not None, tn=tn),
        out_shape=tuple(out_shapes),
        grid=grid, in_specs=in_specs,
        out_specs=tuple(out_specs),
        scratch_shapes=[pltpu.VMEM((tm, d), BF16)],
        compiler_params=_cparams(("parallel", "arbitrary")),
        name="norm_inproj_rope" if rope is not None else "norm_inproj",
    )(*args)


MOBA_HEADS_PER_STEP = 4


def _moba_prompt_kernel(q_ref, k_ref, v_ref, o_ref, kmean_sc, k16_sc, vt_sc, *, nb):
    blk = MOBA_BLOCK
    scale = HEAD_DIM ** -0.5
    hps = MOBA_HEADS_PER_STEP
    hcol = lambda hh: slice(hh * HEAD_DIM, (hh + 1) * HEAD_DIM)
    always = pl.program_id(0) >= 0

    for hh in range(hps):
        for n in range(nb):
            kb = k_ref[n * blk:(n + 1) * blk, hcol(hh)]
            kmean_sc[hh, n:n + 1, :] = jnp.mean(kb, axis=0, keepdims=True)
            k16_sc[hh, n * blk:(n + 1) * blk, :] = kb.astype(BF16)
            vt_sc[hh, :, n * blk:(n + 1) * blk] = v_ref[n * blk:(n + 1) * blk, hcol(hh)].T.astype(BF16)

    for own in range(nb):
        @pl.when(always)
        def _(own=own):
            rows = slice(own * blk, (own + 1) * blk)
            key_i = lax.broadcasted_iota(jnp.int32, (blk, blk), 0)
            qry_i = lax.broadcasted_iota(jnp.int32, (blk, blk), 1)
            causal = key_i <= qry_i
            q16s, selbs = [], []
            for hh in range(hps):
                q = q_ref[rows, hcol(hh)]
                q16s.append(q.astype(BF16))
                if own == 0:
                    selbs.append(None)
                    continue
                s_blk = lax.dot_general(kmean_sc[hh], q, (((1,), (1,)), ((), ())),
                                        precision=lax.Precision.HIGHEST, preferred_element_type=F32)
                n_iota = lax.broadcasted_iota(jnp.int32, s_blk.shape, 0)
                cnt = jnp.zeros(s_blk.shape, F32)
                for m in range(own):
                    s_m = s_blk[m:m + 1, :]
                    cnt = cnt + jnp.where(s_m > s_blk, 1.0, jnp.where((s_m == s_blk) & (n_iota > m), 1.0, 0.0))
                sel = (n_iota < own) & (cnt < float(MOBA_TOPK))
                selbs.append(jnp.where(sel, 0.0, NEG))
            scores = []
            for hh in range(hps):
                sh = []
                for kb in range(own + 1):
                    s = lax.dot_general(k16_sc[hh, kb * blk:(kb + 1) * blk, :], q16s[hh], (((1,), (1,)), ((), ())),
                                        preferred_element_type=F32) * scale
                    sh.append(jnp.where(causal, s, NEG) if kb == own else s + selbs[hh][kb:kb + 1, :])
                scores.append(sh)
            probs, inv_l = [], []
            for hh in range(hps):
                m = functools.reduce(jnp.maximum, [jnp.max(s, axis=0, keepdims=True) for s in scores[hh]])
                ps = [jnp.exp(s - m) for s in scores[hh]]
                l = functools.reduce(jnp.add, [jnp.sum(p, axis=0, keepdims=True) for p in ps])
                inv_l.append(1.0 / l)
                probs.append([p.astype(BF16) for p in ps])
            for hh in range(hps):
                acc = jnp.zeros((HEAD_DIM, blk), F32)
                for kb, p16 in enumerate(probs[hh]):
                    acc = acc + jnp.dot(vt_sc[hh, :, kb * blk:(kb + 1) * blk], p16,
                                        preferred_element_type=F32)
                o_ref[rows, hcol(hh)] = (acc * inv_l[hh]).T


def _moba_prompt(q_arr, k_arr, v_arr, *, batch, seq, n_heads):
    nb = max(-(-seq // MOBA_BLOCK), MOBA_TOPK)
    assert seq % MOBA_BLOCK == 0 and nb * MOBA_BLOCK == seq
    hps = MOBA_HEADS_PER_STEP
    assert n_heads % hps == 0
    cw = hps * HEAD_DIM
    seq_heads = pl.BlockSpec((seq, cw), lambda b, h: (b, h))
    return pl.pallas_call(
        functools.partial(_moba_prompt_kernel, nb=nb),
        out_shape=jax.ShapeDtypeStruct((batch * seq, n_heads * HEAD_DIM), F32),
        grid=(batch, n_heads // hps),
        in_specs=[seq_heads, seq_heads, seq_heads],
        out_specs=seq_heads,
        scratch_shapes=[pltpu.VMEM((hps, nb, HEAD_DIM), F32),
                        pltpu.VMEM((hps, seq, HEAD_DIM), BF16),
                        pltpu.VMEM((hps, HEAD_DIM, seq), BF16)],
        compiler_params=_cparams(("parallel", "parallel")),
        name="moba_prompt",
    )(q_arr, k_arr, v_arr)


def _moba_sample_kernel(*refs, nblk, bps, ppb):
    npg = bps * ppb
    q_ref, kn_ref, vn_ref = refs[1:4]
    k_refs, v_refs = refs[4:4 + npg], refs[4 + npg:4 + 2 * npg]
    o_ref, ksum_sc, m_sc, l_sc, acc_sc = refs[4 + 2 * npg:]
    j = pl.program_id(1)
    scale = HEAD_DIM ** -0.5
    q = q_ref[...]
    q16 = q.astype(BF16)
    n_heads = q.shape[0]
    rows = k_refs[0].shape[0] * n_heads
    lane_head = lax.broadcasted_iota(jnp.int32, (n_heads, rows), 1) % n_heads
    mine = lane_head == lax.broadcasted_iota(jnp.int32, (n_heads, rows), 0)
    scores = []
    for b in range(bps):
        ksum = None
        for k_ref in k_refs[b * ppb:(b + 1) * ppb]:
            kp = k_ref[...]
            kpsum = jnp.sum(kp, axis=0)
            ksum = kpsum if ksum is None else ksum + kpsum
            s = lax.dot_general(q16, kp.reshape(rows, HEAD_DIM).astype(BF16), (((1,), (1,)), ((), ())),
                                preferred_element_type=F32) * scale
            scores.append(jnp.where(mine, s, NEG))
        ksum_sc[j * bps + b] = ksum
    probs = []
    for b in range(bps):
        sb = scores[b * ppb:(b + 1) * ppb]
        m = functools.reduce(jnp.maximum, [jnp.max(s, axis=1, keepdims=True) for s in sb])
        ps = [jnp.exp(s - m) for s in sb]
        l = functools.reduce(jnp.add, [jnp.sum(p, axis=1, keepdims=True) for p in ps])
        m_sc[j * bps + b] = jnp.broadcast_to(m, (n_heads, HEAD_DIM))
        l_sc[j * bps + b] = jnp.broadcast_to(l, (n_heads, HEAD_DIM))
        probs.append([p.astype(BF16) for p in ps])
    for b in range(bps):
        acc = jnp.zeros((n_heads, HEAD_DIM), F32)
        for p16, v_ref in zip(probs[b], v_refs[b * ppb:(b + 1) * ppb]):
            acc = acc + jnp.dot(p16, v_ref[...].reshape(rows, HEAD_DIM).astype(BF16),
                                preferred_element_type=F32)
        acc_sc[j * bps + b] = acc

    @pl.when(j == nblk // bps - 1)
    def _():
        kmean = ksum_sc[...] * (1.0 / MOBA_BLOCK)
        s_blk = jnp.sum(kmean * q[None], axis=-1, keepdims=True)
        n_iota = lax.broadcasted_iota(jnp.int32, s_blk.shape, 0)
        cnt = jnp.zeros(s_blk.shape, F32)
        for mm in range(nblk):
            s_m = s_blk[mm:mm + 1]
            cnt = cnt + jnp.where(s_m > s_blk, 1.0, jnp.where((s_m == s_blk) & (n_iota > mm), 1.0, 0.0))
        sel = cnt < float(MOBA_TOPK)
        m_b = m_sc[...]
        s_own = jnp.sum(q * kn_ref[...], axis=-1, keepdims=True) * scale
        m_tot = jnp.maximum(s_own, jnp.max(jnp.where(sel, m_b, NEG), axis=0))
        w_b = jnp.where(sel, jnp.exp(m_b - m_tot[None]), 0.0)
        w_own = jnp.exp(s_own - m_tot)
        num = w_own * vn_ref[...] + jnp.sum(w_b * acc_sc[...], axis=0)
        den = w_own + jnp.sum(w_b * l_sc[...], axis=0)
        o_ref[...] = num / den


def _moba_sample(q, k_new, v_new, cache_k, cache_v, page_table, layer):
    n, n_heads, _ = q.shape
    page = cache_k.shape[2]
    n_pages = page_table.shape[1]
    assert MOBA_BLOCK % page == 0 and (n_pages * page) % MOBA_BLOCK == 0
    ppb = MOBA_BLOCK // page
    nblk = n_pages * page // MOBA_BLOCK
    assert nblk >= MOBA_TOPK
    bps = next(b for b in (8, 4, 2, 1) if nblk % b == 0)
    npg = bps * ppb
    vec = pl.BlockSpec((None, n_heads, HEAD_DIM), lambda i, j, pt: (i, 0, 0))

    def page_spec(which):
        return pl.BlockSpec((None, None, page, n_heads, HEAD_DIM),
                            lambda i, j, pt: (layer, pt[i, npg * j + which], 0, 0, 0))

    pages = [page_spec(w) for w in range(npg)]
    grid_spec = pltpu.PrefetchScalarGridSpec(
        num_scalar_prefetch=1, grid=(n, nblk // bps),
        in_specs=[vec, vec, vec] + pages + pages,
        out_specs=pl.BlockSpec((None, n_heads, HEAD_DIM), lambda i, j, pt: (i, 0, 0)),
        scratch_shapes=[pltpu.VMEM((nblk, n_heads, HEAD_DIM), F32)] * 4)
    return pl.pallas_call(
        functools.partial(_moba_sample_kernel, nblk=nblk, bps=bps, ppb=ppb),
        out_shape=jax.ShapeDtypeStruct((n, n_heads, HEAD_DIM), F32),
        grid_spec=grid_spec,
        compiler_params=_cparams(("parallel", "arbitrary")),
        name="moba_sample",
    )(page_table, q, k_new, v_new, *([cache_k] * npg), *([cache_v] * npg))


def _s5_kernel(u_ref, x0_ref, bb_ref, cc_ref, ar_ref, ai_ref, d_ref, gw_ref, gb_ref,
               o_ref, st_ref, state_sc, xs_sc, y_sc, *, steps, nb, n_slabs):
    c = pl.program_id(0)
    half = S5_SLAB_GROUPS * S5_STATE
    wslab = S5_SLAB_GROUPS * S5_GROUP
    rows = steps * nb

    @pl.when(c == 0)
    def _():
        state_sc[...] = x0_ref[...]

    for gs in range(n_slabs):
        cols = slice(gs * wslab, (gs + 1) * wslab)
        ug = u_ref[:, :, cols].reshape(rows, wslab)
        xs_sc[...] = jnp.dot(ug.astype(BF16), bb_ref[gs], preferred_element_type=F32)
        ar = jnp.broadcast_to(ar_ref[gs], (nb, half))
        ai = jnp.broadcast_to(ai_ref[gs], (nb, half))

        def step(t, carry):
            xr, xi = carry
            r0 = pl.multiple_of(t * nb, nb)
            nxr = ar * xr - ai * xi + xs_sc[pl.ds(r0, nb), 0:half]
            nxi = ar * xi + ai * xr + xs_sc[pl.ds(r0, nb), half:2 * half]
            xs_sc[pl.ds(r0, nb), 0:half] = nxr
            xs_sc[pl.ds(r0, nb), half:2 * half] = nxi
            return nxr, nxi

        carry = (state_sc[gs, 0], state_sc[gs, 1])
        if steps == 1:
            xr, xi = step(0, carry)
        else:
            xr, xi = lax.fori_loop(0, steps, step, carry, unroll=4)
        state_sc[gs, 0] = xr
        state_sc[gs, 1] = xi
        y = jnp.dot(xs_sc[...].astype(BF16), cc_ref[gs], preferred_element_type=F32)
        y_sc[:, cols] = y + d_ref[:, cols] * ug

    z = jax.nn.gelu(y_sc[...])
    gate = jnp.dot(z.astype(BF16), gw_ref[...], preferred_element_type=F32) + gb_ref[...]
    o_ref[...] = (z * jax.nn.sigmoid(gate)).reshape(o_ref.shape)

    @pl.when(c == pl.num_programs(0) - 1)
    def _():
        st_ref[...] = state_sc[...]


def _s5(u_tb, x0, bb, cc, ar, ai, d, gw, gb, *, steps):
    t, nb, w = u_tb.shape
    n_slabs = bb.shape[0]
    half = S5_SLAB_GROUPS * S5_STATE
    rows = steps * nb
    const = lambda *shape: pl.BlockSpec(shape, lambda c: (0,) * len(shape))
    return pl.pallas_call(
        functools.partial(_s5_kernel, steps=steps, nb=nb, n_slabs=n_slabs),
        out_shape=(jax.ShapeDtypeStruct((t, nb, w), F32),
                   jax.ShapeDtypeStruct((n_slabs, 2, nb, half), F32)),
        grid=(t // steps,),
        in_specs=[pl.BlockSpec((steps, nb, w), lambda c: (c, 0, 0)),
                  const(n_slabs, 2, nb, half), const(*bb.shape), const(*cc.shape),
                  const(*ar.shape), const(*ai.shape), const(1, w), const(w, w), const(1, w)],
        out_specs=(pl.BlockSpec((steps, nb, w), lambda c: (c, 0, 0)), const(n_slabs, 2, nb, half)),
        scratch_shapes=[pltpu.VMEM((n_slabs, 2, nb, half), F32),
                        pltpu.VMEM((rows, 2 * half), F32),
                        pltpu.VMEM((rows, w), F32)],
        compiler_params=_cparams(("arbitrary",)),
        name="s5_scan",
    )(u_tb, x0, bb, cc, ar, ai, d, gw, gb)


def _s5_params(lam_re, lam_im, log_dt, b_ri, c_ri):
    g, p = lam_re.shape
    ns = g // S5_SLAB_GROUPS
    dt = jnp.exp(log_dt.astype(F32))[:, None]
    lr, li = lam_re.astype(F32), lam_im.astype(F32)
    mag = jnp.exp(lr * dt)
    ar, ai = mag * jnp.cos(li * dt), mag * jnp.sin(li * dt)
    d2 = lr * lr + li * li
    cr = ((ar - 1.0) * lr + ai * li) / d2
    ci = (ai * lr - (ar - 1.0) * li) / d2
    br, bi = b_ri[..., 0].astype(F32), b_ri[..., 1].astype(F32)
    bbr = cr[..., None] * br - ci[..., None] * bi
    bbi = cr[..., None] * bi + ci[..., None] * br
    eye = jnp.eye(S5_SLAB_GROUPS, dtype=F32)

    def blockdiag_in(m):
        m = m.reshape(ns, S5_SLAB_GROUPS, p, S5_GROUP)
        return jnp.einsum('sipc,ij->sicjp', m, eye).reshape(ns, S5_SLAB_GROUPS * S5_GROUP, S5_SLAB_GROUPS * p)

    def blockdiag_out(m):
        m = m.reshape(ns, S5_SLAB_GROUPS, S5_GROUP, p)
        return jnp.einsum('sicp,ij->sipjc', m, eye).reshape(ns, S5_SLAB_GROUPS * p, S5_SLAB_GROUPS * S5_GROUP)

    bb = jnp.concatenate([blockdiag_in(bbr), blockdiag_in(bbi)], axis=-1).astype(BF16)
    c_re, c_im = c_ri[..., 0].astype(F32), c_ri[..., 1].astype(F32)
    cc = jnp.concatenate([blockdiag_out(c_re), blockdiag_out(-c_im)], axis=1).astype(BF16)
    ar_s = ar.reshape(ns, 1, S5_SLAB_GROUPS * p)
    ai_s = ai.reshape(ns, 1, S5_SLAB_GROUPS * p)
    return bb, cc, ar_s, ai_s


def _s5_state_to_slabs(x0):
    n, g, p, _ = x0.shape
    ns = g // S5_SLAB_GROUPS
    return x0.reshape(n, ns, S5_SLAB_GROUPS * p, 2).transpose(1, 3, 0, 2)


def _s5_state_from_slabs(st, g):
    ns, _, n, hp = st.shape
    return st.transpose(2, 0, 3, 1).reshape(n, g, hp // S5_SLAB_GROUPS, 2)


def _outproj_even_kernel(att_ref, ga_ref, so_ref, gb_ref, w_ref, res_ref, o_ref):
    wa = att_ref.shape[1]
    a0 = (att_ref[...] * _silu(ga_ref[...])).astype(BF16)
    a1 = (so_ref[...] * _silu(gb_ref[...])).astype(BF16)
    o_ref[...] = (res_ref[...] + jnp.dot(a0, w_ref[:wa, :], preferred_element_type=F32)
                  + jnp.dot(a1, w_ref[wa:, :], preferred_element_type=F32))


def _outproj_even(att, gates, so2d, so_index, w, res, *, ga_col, gb_col, tm):
    n, wa = att.shape
    d = w.shape[1]
    gac, gbc = ga_col // wa, gb_col // wa
    so_map = (lambda i: (i, 0)) if so_index is None else so_index
    return pl.pallas_call(
        _outproj_even_kernel,
        out_shape=jax.ShapeDtypeStruct((n, d), F32),
        grid=(n // tm,),
        in_specs=[pl.BlockSpec((tm, wa), lambda i: (i, 0)),
                  pl.BlockSpec((tm, wa), lambda i: (i, gac)),
                  pl.BlockSpec((tm, wa), so_map),
                  pl.BlockSpec((tm, wa), lambda i: (i, gbc)),
                  pl.BlockSpec(w.shape, lambda i: (0, 0), pipeline_mode=pl.Buffered(1)),
                  pl.BlockSpec((tm, d), lambda i: (i, 0))],
        out_specs=pl.BlockSpec((tm, d), lambda i: (i, 0)),
        compiler_params=_cparams(("parallel",)),
        name="outproj_even",
    )(att, gates, so2d, gates, w, res)


def _matmul_res_kernel(a_ref, w_ref, res_ref, o_ref):
    o_ref[...] = res_ref[...] + jnp.dot(a_ref[...], w_ref[...], preferred_element_type=F32)


def _matmul_res(a, w, res, *, tm):
    n, k = a.shape
    d = w.shape[1]
    return pl.pallas_call(
        _matmul_res_kernel,
        out_shape=jax.ShapeDtypeStruct((n, d), F32),
        grid=(n // tm,),
        in_specs=[pl.BlockSpec((tm, k), lambda i: (i, 0)),
                  pl.BlockSpec((k, d), lambda i: (0, 0), pipeline_mode=pl.Buffered(1)),
                  pl.BlockSpec((tm, d), lambda i: (i, 0))],
        out_specs=pl.BlockSpec((tm, d), lambda i: (i, 0)),
        compiler_params=_cparams(("parallel",)),
        name="outproj_odd",
    )(a, w, res)


def _softplus(x):
    return jnp.maximum(x, 0.0) + jnp.log1p(jnp.exp(-jnp.abs(x)))


def _ssd_prompt_kernel(z_ref, x_ref, b_ref, c_ref, dt_ref, cw_ref, cb_ref, dtb_ref, alog_ref, de_ref,
                       nw_ref, e2_ref, cum_ref, y_ref, st_ref, xpad_sc, act_sc, ht_sc, y_sc, *, chunk, width):
    c = pl.program_id(1)
    L = chunk
    hist = 8
    gw = width // SSD_GROUPS
    hpg = gw // SSD_HEAD_DIM
    bw = SSD_GROUPS * SSD_STATE
    pw = 2 * SSD_HEAD_DIM
    assert pw == 128 and hpg % 2 == 0

    @pl.when(c == 0)
    def _():
        xpad_sc[0:hist, :] = jnp.zeros((hist, xpad_sc.shape[1]), F32)
        ht_sc[...] = jnp.zeros(ht_sc.shape, F32)

    xpad_sc[hist:hist + L, 0:width] = x_ref[...]
    xpad_sc[hist:hist + L, width:width + bw] = b_ref[...]
    xpad_sc[hist:hist + L, width + bw:width + 2 * bw] = c_ref[...]
    cs = 512
    for j in range(xpad_sc.shape[1] // cs):
        sl = slice(j * cs, (j + 1) * cs)
        ext = xpad_sc[:, sl]
        conv = cw_ref[0:1, sl] * ext
        for k in range(1, SSD_CONV):
            conv = cw_ref[k:k + 1, sl] * ext + pltpu.roll(conv, 1, 0)
        act_sc[:, sl] = _silu(conv[hist:, :] + cb_ref[:, sl])
    xpad_sc[0:hist, :] = xpad_sc[L:L + hist, :]

    dt = _softplus(dt_ref[...] + dtb_ref[...])
    a = -jnp.exp(alog_ref[...])
    adt = dt * a
    a_cs_t = _dot3(adt.T, cum_ref[...])
    a_cs = a_cs_t.T
    ea = jnp.exp(a_cs)
    ds = jnp.exp(a_cs[L - 1:L, :] - a_cs)
    row_i = lax.broadcasted_iota(jnp.int32, (L, L), 0)
    col_i = lax.broadcasted_iota(jnp.int32, (L, L), 1)
    tri = row_i >= col_i
    first_head = lax.broadcasted_iota(jnp.int32, (L, pw), 1) < SSD_HEAD_DIM

    def hi_mid(v):
        hi, mid, _ = _split3(v)
        return jnp.concatenate([hi, mid], axis=1)

    dt_p, ea_p, ds_p = hi_mid(dt), hi_mid(ea), hi_mid(ds)

    for g in range(SSD_GROUPS):
        gsl = slice(g * gw, (g + 1) * gw)
        e2_g = e2_ref[:, gsl]
        dt_g = jnp.dot(dt_p, e2_g, preferred_element_type=F32)
        ea_g = jnp.dot(ea_p, e2_g, preferred_element_type=F32)
        ds_g = jnp.dot(ds_p, e2_g, preferred_element_type=F32)
        xs_g = act_sc[:, gsl]
        xdt_g = xs_g * dt_g
        b_g = act_sc[:, width + g * SSD_STATE:width + (g + 1) * SSD_STATE].astype(BF16)
        c_g = act_sc[:, width + bw + g * SSD_STATE:width + bw + (g + 1) * SSD_STATE].astype(BF16)
        cb = lax.dot_general(c_g, b_g, (((1,), (1,)), ((), ())), preferred_element_type=F32)
        cb = jnp.where(tri, cb, 0.0)
        h_prev = ht_sc[g]
        y_off = jnp.dot(c_g, h_prev.astype(BF16), preferred_element_type=F32) * ea_g
        states_t = lax.dot_general(b_g, (xdt_g * ds_g).astype(BF16), (((0,), (0,)), ((), ())),
                                   preferred_element_type=F32)
        ht_sc[g] = h_prev * ea_g[L - 1:L, :] + states_t
        xdt16 = xdt_g.astype(BF16)
        for pr in range(hpg // 2):
            psl = slice(pr * pw, (pr + 1) * pw)
            decays = []
            for h in (g * hpg + 2 * pr, g * hpg + 2 * pr + 1):
                diff = a_cs[:, h:h + 1] - a_cs_t[h:h + 1, :]
                decays.append((cb * jnp.exp(jnp.minimum(diff, 0.0))).astype(BF16))
            xp = xdt16[:, psl]
            zero = jnp.zeros_like(xp)
            rhs = jnp.concatenate([jnp.where(first_head, xp, zero), jnp.where(first_head, zero, xp)], axis=0)
            yd = jnp.dot(jnp.concatenate(decays, axis=1), rhs, preferred_element_type=F32)
            y_sc[:, psl] = yd + y_off[:, psl] + de_ref[:, g * gw + pr * pw:g * gw + (pr + 1) * pw] * xs_g[:, psl]
        y = y_sc[...] * _silu(z_ref[:, gsl])
        ms = jnp.mean(y * y, axis=-1, keepdims=True)
        y_ref[:, gsl] = (y * lax.rsqrt(ms + EPS) * nw_ref[:, gsl]).astype(y_ref.dtype)

    @pl.when(c == pl.num_programs(1) - 1)
    def _():
        for g in range(SSD_GROUPS):
            st_ref[g * gw:(g + 1) * gw, :] = ht_sc[g].T


def _ssd_prompt(proj, dt_raw, cw, cb, dtb, alog, de, nw, e2_mat, tril, *, batch, seq, width):
    L = SSD_CHUNK
    assert seq % L == 0
    nc = seq // L
    bw = SSD_GROUPS * SSD_STATE
    conv_dim = width + 2 * bw
    xcol = width // width
    bcol = (2 * width) // bw
    ccol = (2 * width + bw) // bw
    const = lambda *shape: pl.BlockSpec(shape, lambda b, c: (0,) * len(shape))
    return pl.pallas_call(
        functools.partial(_ssd_prompt_kernel, chunk=L, width=width),
        out_shape=(jax.ShapeDtypeStruct((batch * seq, width), BF16),
                   jax.ShapeDtypeStruct((batch, width, SSD_STATE), F32)),
        grid=(batch, nc),
        in_specs=[pl.BlockSpec((L, width), lambda b, c: (b * nc + c, 0)),
                  pl.BlockSpec((L, width), lambda b, c: (b * nc + c, xcol)),
                  pl.BlockSpec((L, bw), lambda b, c: (b * nc + c, bcol)),
                  pl.BlockSpec((L, bw), lambda b, c: (b * nc + c, ccol)),
                  pl.BlockSpec((L, 128), lambda b, c: (b * nc + c, 0)),
                  const(SSD_CONV, conv_dim), const(1, conv_dim), const(1, 128), const(1, 128),
                  const(1, width), const(1, width), const(256, width), const(L, L)],
        out_specs=(pl.BlockSpec((L, width), lambda b, c: (b * nc + c, 0)),
                   pl.BlockSpec((None, width, SSD_STATE), lambda b, c: (b, 0, 0))),
        scratch_shapes=[pltpu.VMEM((L + 8, conv_dim), F32),
                        pltpu.VMEM((L, conv_dim), F32),
                        pltpu.VMEM((SSD_GROUPS, SSD_STATE, width // SSD_GROUPS), F32),
                        pltpu.VMEM((L, width // SSD_GROUPS), F32)],
        compiler_params=_cparams(("parallel", "arbitrary")),
        name="ssd_prompt",
    )(proj, proj, proj, proj, dt_raw, cw, cb, dtb, alog, de, nw, e2_mat, tril)


def _conv_step_kernel(x_ref, s0_ref, s1_ref, s2_ref, cw_ref, cb_ref, dt_ref, dtb_ref, alog_ref,
                      act_ref, dto_ref, dao_ref):
    conv = (cb_ref[...] + cw_ref[0:1, :] * s0_ref[...] + cw_ref[1:2, :] * s1_ref[...]
            + cw_ref[2:3, :] * s2_ref[...] + cw_ref[3:4, :] * x_ref[...])
    act_ref[...] = _silu(conv)

    @pl.when(pl.program_id(0) == 0)
    def _():
        dt = _softplus(dt_ref[...] + dtb_ref[...])
        dto_ref[...] = dt
        dao_ref[...] = jnp.exp(dt * -jnp.exp(alog_ref[...]))


def _conv_step(proj, dt_raw, conv_state2d, cw, cb, dtb, alog, *, width):
    n = proj.shape[0]
    conv_dim = cw.shape[1]
    cs = 512
    nj = conv_dim // cs
    x0 = width // cs
    return pl.pallas_call(
        _conv_step_kernel,
        out_shape=(jax.ShapeDtypeStruct((n, conv_dim), F32), jax.ShapeDtypeStruct((n, 128), F32),
                   jax.ShapeDtypeStruct((n, 128), F32)),
        grid=(nj,),
        in_specs=[pl.BlockSpec((n, cs), lambda j: (0, x0 + j)),
                  pl.BlockSpec((n, cs), lambda j: (0, j)),
                  pl.BlockSpec((n, cs), lambda j: (0, nj + j)),
                  pl.BlockSpec((n, cs), lambda j: (0, 2 * nj + j)),
                  pl.BlockSpec((SSD_CONV, cs), lambda j: (0, j)),
                  pl.BlockSpec((1, cs), lambda j: (0, j)),
                  pl.BlockSpec((n, 128), lambda j: (0, 0)),
                  pl.BlockSpec((1, 128), lambda j: (0, 0)),
                  pl.BlockSpec((1, 128), lambda j: (0, 0))],
        out_specs=(pl.BlockSpec((n, cs), lambda j: (0, j)),
                   pl.BlockSpec((n, 128), lambda j: (0, 0)),
                   pl.BlockSpec((n, 128), lambda j: (0, 0))),
        compiler_params=_cparams(("arbitrary",)),
        name="ssd_conv_step",
    )(proj, conv_state2d, conv_state2d, conv_state2d, cw, cb, dt_raw, dtb, alog)


def _ssd_step_kernel(dt_sm, da_sm, h0_ref, xt_ref, b_ref, c_ref, z_ref, x_ref, de_ref, nw_ref,
                     hn_ref, y_ref, *, width, per_step):
    gw = width // SSD_GROUPS
    hpg = gw // SSD_HEAD_DIM
    nrow = xt_ref.shape[1]
    row_i = lax.broadcasted_iota(jnp.int32, (nrow, SSD_STATE), 0)
    for s in range(per_step):
        n = pl.program_id(0) * per_step + s
        rowsel = row_i == n
        for g in range(SSD_GROUPS):
            ssl = slice(g * SSD_STATE, (g + 1) * SSD_STATE)
            gsl = slice(g * gw, (g + 1) * gw)
            rhs = jnp.where(rowsel, b_ref[s, :, ssl], 0.0).astype(BF16)
            outer = jnp.dot(xt_ref[gsl, :], rhs, preferred_element_type=F32)
            for r in range(hpg):
                h = g * hpg + r
                rsl = slice(g * gw + r * SSD_HEAD_DIM, g * gw + (r + 1) * SSD_HEAD_DIM)
                hn_ref[s, rsl, :] = (da_sm[n, h] * h0_ref[s, rsl, :]
                                     + dt_sm[n, h] * outer[r * SSD_HEAD_DIM:(r + 1) * SSD_HEAD_DIM, :])
            c8 = jnp.broadcast_to(c_ref[s, :, ssl], (8, SSD_STATE)).astype(BF16)
            yg = lax.dot_general(c8, hn_ref[s, gsl, :].astype(BF16), (((1,), (1,)), ((), ())),
                                 preferred_element_type=F32)[0:1, :]
            y = (yg + de_ref[:, gsl] * x_ref[s, :, gsl]) * _silu(z_ref[s, :, gsl])
            ms = jnp.mean(y * y, axis=-1, keepdims=True)
            y_ref[s, :, gsl] = y * lax.rsqrt(ms + EPS) * nw_ref[:, gsl]


def _ssd_step(dt, da, h0, xt, bm, cm, z, xs, de, nw, *, width):
    n = h0.shape[0]
    bw = SSD_GROUPS * SSD_STATE
    per_step = 2 if n % 2 == 0 else 1
    row = lambda w: pl.BlockSpec((per_step, 1, w), lambda i: (i, 0, 0))
    state = pl.BlockSpec((per_step, width, SSD_STATE), lambda i: (i, 0, 0))
    smem = pl.BlockSpec(memory_space=pltpu.SMEM)
    return pl.pallas_call(
        functools.partial(_ssd_step_kernel, width=width, per_step=per_step),
        out_shape=(jax.ShapeDtypeStruct(h0.shape, F32), jax.ShapeDtypeStruct((n, 1, width), F32)),
        grid=(n // per_step,),
        in_specs=[smem, smem, state,
                  pl.BlockSpec((width, n), lambda i: (0, 0)),
                  row(bw), row(bw), row(width), row(width),
                  pl.BlockSpec((1, width), lambda i: (0, 0)),
                  pl.BlockSpec((1, width), lambda i: (0, 0))],
        out_specs=(state, row(width)),
        compiler_params=_cparams(("arbitrary",)),
        name="ssd_step",
    )(dt, da, h0, xt, bm, cm, z, xs, de, nw)


def _rope_tables(pos):
    half = HEAD_DIM // 2
    inv = ROPE_THETA ** (-jnp.arange(half, dtype=F32) / half)
    ang = pos.astype(F32)[:, None] * inv[None, :]
    cos, sin = jnp.cos(ang), jnp.sin(ang)
    return jnp.concatenate([cos, cos], axis=-1), jnp.concatenate([-sin, sin], axis=-1)


def _even_layer(yp, ys, e, norm_w_i, cache_k, cache_v, page_table, state_s5, w_in, qn, kn,
                lam_re, lam_im, log_dt, s5_b, s5_c, s5_d, glu_w, glu_b, w_out):
    batch, seq, d = yp.shape
    nsamp = ys.shape[0]
    assert ys.shape[1] == 1
    aw = d // 2
    n_heads = aw // HEAD_DIM
    n_past = page_table.shape[1] * cache_k.shape[2]
    groups = lam_re.shape[0]

    w_in16 = w_in.astype(BF16)
    w_out16 = w_out.astype(BF16)
    nw = norm_w_i.reshape(1, d)
    qn2, kn2 = qn.reshape(1, HEAD_DIM), kn.reshape(1, HEAD_DIM)
    cos_p, sin_p = _rope_tables(jnp.arange(seq, dtype=jnp.int32))
    cos_s, sin_s = _rope_tables(jnp.full((nsamp,), n_past, dtype=jnp.int32))

    xp2 = yp.reshape(batch * seq, d)
    xs2 = ys.reshape(nsamp, d)
    tn = 1024
    nt = aw // tn
    routes = [(0, 0, nt, "q"), (1, nt, 2 * nt, "k"), (2, 2 * nt, 3 * nt, "plain"),
              (0, 3 * nt, 4 * nt, "plain"), (3, 4 * nt, 5 * nt, "plain"), (0, 5 * nt, 6 * nt, "plain")]
    plain_map = lambda i, c: (i, c)
    tm_p = min(1024, seq)
    nper = seq // tm_p
    n_p = batch * seq
    outs_p = [((n_p, 3 * aw), plain_map), ((n_p, aw), plain_map), ((n_p, aw), plain_map),
              ((seq, batch * aw), lambda i, c: (i % nper, (i // nper) * nt + c))]
    outs_s = [((nsamp, 3 * aw), plain_map), ((nsamp, aw), plain_map), ((nsamp, aw), plain_map),
              ((nsamp, aw), plain_map)]
    qg_p, k_p, v_p, u_p = _norm_inproj(xp2, nw, w_in16, tm=tm_p, tn=tn, routes=routes, outs=outs_p,
                                       rope=(cos_p, sin_p, qn2, kn2), nper=nper)
    qg_s, k_s, v_s, u_s = _norm_inproj(xs2, nw, w_in16, tm=nsamp, tn=tn, routes=routes, outs=outs_s,
                                       rope=(cos_s, sin_s, qn2, kn2))

    att_p = _moba_prompt(qg_p, k_p, v_p, batch=batch, seq=seq, n_heads=n_heads)
    heads3 = lambda a: a.reshape(nsamp, n_heads, HEAD_DIM)
    att_s = _moba_sample(heads3(qg_s[:, :aw]), heads3(k_s), heads3(v_s), cache_k, cache_v, page_table,
                         e).reshape(nsamp, aw)

    bb, cc, ar, ai = _s5_params(lam_re, lam_im, log_dt, s5_b, s5_c)
    d2 = s5_d.reshape(1, aw).astype(F32)
    gw16 = glu_w.astype(BF16)
    gb2 = glu_b.reshape(1, aw).astype(F32)
    zero_state = jnp.zeros((groups // S5_SLAB_GROUPS, 2, batch, S5_SLAB_GROUPS * S5_STATE), F32)
    so_p, st_p = _s5(u_p.reshape(seq, batch, aw), zero_state, bb, cc, ar, ai, d2, gw16, gb2, steps=64)
    so_s, st_s = _s5(u_s.reshape(1, nsamp, aw), _s5_state_to_slabs(state_s5), bb, cc, ar, ai, d2, gw16, gb2,
                     steps=1)

    tm_o = 512
    nto = seq // tm_o
    yp_new = _outproj_even(att_p, qg_p, so_p.reshape(seq, batch * aw), lambda i: (i % nto, i // nto),
                           w_out16, xp2, ga_col=aw, gb_col=2 * aw, tm=tm_o)
    ys_new = _outproj_even(att_s, qg_s, so_s.reshape(nsamp, aw), None, w_out16, xs2,
                           ga_col=aw, gb_col=2 * aw, tm=nsamp)

    heads4 = lambda a, b, t: a.reshape(b, t, n_heads, HEAD_DIM)
    outs = (heads4(k_p, batch, seq), heads4(v_p, batch, seq), heads4(k_s, nsamp, 1), heads4(v_s, nsamp, 1),
            _s5_state_from_slabs(st_p, groups), _s5_state_from_slabs(st_s, groups))
    return yp_new.reshape(batch, seq, d), ys_new.reshape(nsamp, 1, d), outs


def _odd_layer(yp, ys, norm_w_i, state_conv, state_ssd, w_in, conv_w, conv_b, dt_bias, a_log, d_skip,
               ssd_norm_w, w_out):
    batch, seq, d = yp.shape
    nsamp = ys.shape[0]
    n_heads = a_log.shape[0]
    width = n_heads * SSD_HEAD_DIM
    bw = SSD_GROUPS * SSD_STATE
    conv_dim = width + 2 * bw
    odd_in = w_in.shape[1]
    assert odd_in == width + conv_dim + n_heads and n_heads <= 128
    nmain = width + conv_dim
    tn = _largest_tile(nmain, 1536, 256)

    w_main16 = w_in[:, :nmain].astype(BF16)
    w_dt16 = jnp.pad(w_in[:, nmain:].astype(BF16), ((0, 0), (0, 128 - n_heads)))
    w_out16 = w_out.astype(BF16)
    nw = norm_w_i.reshape(1, d)
    pad128 = lambda v: jnp.pad(v.astype(F32), (0, 128 - n_heads)).reshape(1, 128)
    dtb, alog = pad128(dt_bias), pad128(a_log)
    de = jnp.repeat(d_skip.astype(F32), SSD_HEAD_DIM).reshape(1, width)
    gnw = ssd_norm_w.reshape(1, width).astype(F32)
    cb = conv_b.reshape(1, conv_dim).astype(F32)
    cw = conv_w.astype(F32)
    e_mat = (jnp.arange(128)[:, None] == (jnp.arange(width)[None, :] // SSD_HEAD_DIM)).astype(BF16)
    tril = (jnp.arange(SSD_CHUNK)[:, None] <= jnp.arange(SSD_CHUNK)[None, :]).astype(BF16)

    xp2 = yp.reshape(batch * seq, d)
    xs2 = ys.reshape(nsamp, d)
    plain_map = lambda i, c: (i, c)
    routes = [(0, 0, nmain // tn, "plain")]
    proj_p, dtr_p = _norm_inproj(xp2, nw, w_main16, tm=min(1024, seq), tn=tn, routes=routes,
                                 outs=[((batch * seq, nmain), plain_map)], side_w=w_dt16)
    proj_s, dtr_s = _norm_inproj(xs2, nw, w_main16, tm=nsamp, tn=tn, routes=routes,
                                 outs=[((nsamp, nmain), plain_map)], side_w=w_dt16)

    e2_mat = jnp.concatenate([e_mat, e_mat], axis=0)
    yn_p, st_p = _ssd_prompt(proj_p, dtr_p, cw, cb, dtb, alog, de, gnw, e2_mat, tril, batch=batch, seq=seq,
                             width=width)
    yp_new = _matmul_res(yn_p, w_out16, xp2, tm=512)

    act_s, dt_s, da_s = _conv_step(proj_s, dtr_s, state_conv.reshape(nsamp, (SSD_CONV - 1) * conv_dim), cw, cb,
                                   dtb, alog, width=width)
    xs_s = act_s[:, :width]
    hn_s, yn_s = _ssd_step(dt_s, da_s, state_ssd.reshape(nsamp, width, SSD_STATE), xs_s.T.astype(BF16),
                           act_s[:, width:width + bw].reshape(nsamp, 1, bw),
                           act_s[:, width + bw:].reshape(nsamp, 1, bw),
                           proj_s[:, :width].reshape(nsamp, 1, width), xs_s.reshape(nsamp, 1, width),
                           de, gnw, width=width)
    ys_new = _matmul_res(yn_s.reshape(nsamp, width).astype(BF16), w_out16, xs2, tm=nsamp)

    buf_p = proj_p.reshape(batch, seq, nmain)[:, seq - (SSD_CONV - 1):, width:width + conv_dim]
    buf_s = jnp.concatenate([state_conv[:, 1:, :], proj_s[:, width:width + conv_dim].reshape(nsamp, 1, conv_dim)], axis=1)
    outs = (buf_p, buf_s, st_p.reshape(batch, n_heads, SSD_HEAD_DIM, SSD_STATE),
            hn_s.reshape(nsamp, n_heads, SSD_HEAD_DIM, SSD_STATE))
    return yp_new.reshape(batch, seq, d), ys_new.reshape(nsamp, 1, d), outs


def kernel(x_prompt, x_sample, cache_k, cache_v, page_table, state_s5, state_conv, state_ssd, norm_w, w_in_even, q_norm_w, k_norm_w, s5_lambda_re, s5_lambda_im, s5_log_dt, s5_b, s5_c, s5_d, s5_glu_w, s5_glu_b, w_out_even, w_in_odd, conv_w, conv_b, ssd_dt_bias, ssd_a_log, ssd_d, ssd_norm_w, w_out_odd):
    depth = norm_w.shape[0]
    yp, ys = x_prompt, x_sample
    even_outs, odd_outs = [], []
    for i in range(depth):
        if i % 2 == 0:
            e = i // 2
            yp, ys, outs = _even_layer(yp, ys, e, norm_w[i], cache_k, cache_v, page_table, state_s5[e],
                                       w_in_even[e], q_norm_w[e], k_norm_w[e], s5_lambda_re[e], s5_lambda_im[e],
                                       s5_log_dt[e], s5_b[e], s5_c[e], s5_d[e], s5_glu_w[e], s5_glu_b[e], w_out_even[e])
            even_outs.append(outs)
        else:
            o = i // 2
            yp, ys, outs = _odd_layer(yp, ys, norm_w[i], state_conv[o], state_ssd[o], w_in_odd[o], conv_w[o],
                                      conv_b[o], ssd_dt_bias[o], ssd_a_log[o], ssd_d[o], ssd_norm_w[o], w_out_odd[o])
            odd_outs.append(outs)
    stack = lambda outs, k: jnp.stack([o[k] for o in outs])
    return (yp, ys, stack(even_outs, 0), stack(even_outs, 1), stack(even_outs, 2), stack(even_outs, 3),
            stack(even_outs, 4), stack(even_outs, 5), stack(odd_outs, 0), stack(odd_outs, 1),
            stack(odd_outs, 2), stack(odd_outs, 3))
```

```python
import functools

import jax
import jax.numpy as jnp
from jax import lax
from jax.experimental import pallas as pl
from jax.experimental.pallas import tpu as pltpu

F32 = jnp.float32
BF16 = jnp.bfloat16

HEAD_DIM = 128
MOBA_BLOCK = 256
MOBA_TOPK = 3
ROPE_THETA = 10000.0
S5_GROUP = 16
S5_STATE = 64
S5_SLAB_GROUPS = 8
SSD_HEAD_DIM = 64
SSD_GROUPS = 8
SSD_STATE = 128
SSD_CONV = 4
SSD_CHUNK = 128
EPS = 1e-6
NEG = -1e30
VMEM_LIMIT = 58 * 1024 * 1024


def _cparams(sem):
    return pltpu.CompilerParams(dimension_semantics=sem, vmem_limit_bytes=VMEM_LIMIT)


def _silu(x):
    hx = 0.5 * x
    return hx + hx * jnp.tanh(hx)


def _split3(v):
    hi = v.astype(BF16)
    r1 = v - hi.astype(F32)
    mid = r1.astype(BF16)
    lo = (r1 - mid.astype(F32)).astype(BF16)
    return hi, mid, lo


def _dot3(v, m):
    hi, mid, lo = _split3(v)
    return (jnp.dot(hi, m, preferred_element_type=F32) + jnp.dot(mid, m, preferred_element_type=F32)
            + jnp.dot(lo, m, preferred_element_type=F32))


def _norm_inproj_kernel(*refs, routes, has_rope, has_side, tn):
    n_out = 1 + max(r[0] for r in routes) + (1 if has_side else 0)
    ins, outs, h_sc = refs[:-(n_out + 1)], refs[-(n_out + 1):-1], refs[-1]
    if has_side:
        ins, ws_ref, side_ref = ins[:-1], ins[-1], outs[-1]
    if has_rope:
        x_ref, nw_ref, w_ref, cos_ref, sin_ref, qn_ref, kn_ref = ins
    else:
        x_ref, nw_ref, w_ref = ins
    j = pl.program_id(1)

    @pl.when(j == 0)
    def _():
        x = x_ref[...]
        ms = jnp.mean(x * x, axis=-1, keepdims=True)
        h_sc[...] = (x * lax.rsqrt(ms + EPS) * nw_ref[...]).astype(BF16)
        if has_side:
            side_ref[...] = jnp.dot(h_sc[...], ws_ref[...], preferred_element_type=F32)

    acc = jnp.dot(h_sc[...], w_ref[...], preferred_element_type=F32)
    for out_idx, j0, j1, kind in routes:
        o_ref = outs[out_idx]

        @pl.when((j >= j0) & (j < j1))
        def _(o_ref=o_ref, kind=kind):
            if kind == "plain":
                o_ref[...] = acc
                return
            hw = qn_ref[...] if kind == "q" else kn_ref[...]
            cos = cos_ref[...]
            sin = sin_ref[...]
            for hh in range(tn // HEAD_DIM):
                sl = slice(hh * HEAD_DIM, (hh + 1) * HEAD_DIM)
                a = acc[:, sl]
                ms = jnp.mean(a * a, axis=-1, keepdims=True)
                a = a * lax.rsqrt(ms + EPS) * hw
                o_ref[:, sl] = a * cos + pltpu.roll(a, HEAD_DIM // 2, 1) * sin


def _largest_tile(n, cap, unit):
    return max(t for t in range(unit, cap + 1, unit) if n % t == 0)


def _owned_col(j, ranges):
    total = sum(j1 - j0 for j0, j1 in ranges)
    cnt = sum(jnp.clip(j - j0, 0, j1 - j0) for j0, j1 in ranges)
    return jnp.minimum(cnt, total - 1)


def _norm_inproj(x2d, nw, w, *, tm, tn, routes, outs, rope=None, nper=1, side_w=None):
    n, d = x2d.shape
    nout = sum(j1 - j0 for _, j0, j1, _ in routes) * tn
    assert nout <= w.shape[1]
    grid = (n // tm, nout // tn)
    in_specs = [pl.BlockSpec((tm, d), lambda i, j: (i, 0)),
                pl.BlockSpec((1, d), lambda i, j: (0, 0)),
                pl.BlockSpec((d, tn), lambda i, j: (0, j))]
    args = [x2d, nw, w]
    if rope is not None:
        cos2, sin2, qn, kn = rope
        in_specs += [pl.BlockSpec((tm, HEAD_DIM), lambda i, j: (i % nper, 0), pipeline_mode=pl.Buffered(1)),
                     pl.BlockSpec((tm, HEAD_DIM), lambda i, j: (i % nper, 0), pipeline_mode=pl.Buffered(1)),
                     pl.BlockSpec((1, HEAD_DIM), lambda i, j: (0, 0)),
                     pl.BlockSpec((1, HEAD_DIM), lambda i, j: (0, 0))]
        args += [cos2, sin2, qn, kn]
    out_shapes, out_specs = [], []
    for k, (shape, block_map) in enumerate(outs):
        ranges = [(j0, j1) for idx, j0, j1, _ in routes if idx == k]
        out_shapes.append(jax.ShapeDtypeStruct(shape, F32))
        single = sum(j1 - j0 for j0, j1 in ranges) == 1 and grid[1] > 2
        out_specs.append(pl.BlockSpec(
            (tm, tn), lambda i, j, ranges=ranges, block_map=block_map: block_map(i, _owned_col(j, ranges)),
            **({"pipeline_mode": pl.Buffered(1)} if single else {})))
    if side_w is not None:
        in_specs.append(pl.BlockSpec(side_w.shape, lambda i, j: (0, 0)))
        args.append(side_w)
        out_shapes.append(jax.ShapeDtypeStruct((n, side_w.shape[1]), F32))
        out_specs.append(pl.BlockSpec((tm, side_w.shape[1]), lambda i, j: (i, 0)))
    return pl.pallas_call(
        functools.partial(_norm_inproj_kernel, routes=tuple(routes), has_rope=rope is not None,
                          has_side=side_w is not None, tn=tn),
        out_shape=tuple(out_shapes),
        grid=grid, in_specs=in_specs,
        out_specs=tuple(out_specs),
        scratch_shapes=[pltpu.VMEM((tm, d), BF16)],
        compiler_params=_cparams(("parallel", "arbitrary")),
        name="norm_inproj_rope" if rope is not None else "norm_inproj",
    )(*args)


MOBA_HEADS_PER_STEP = 4


def _moba_prompt_kernel(q_ref, k_ref, v_ref, o_ref, kmean_sc, k16_sc, vt_sc, *, nb):
    blk = MOBA_BLOCK
    scale = HEAD_DIM ** -0.5
    hps = MOBA_HEADS_PER_STEP
    hcol = lambda hh: slice(hh * HEAD_DIM, (hh + 1) * HEAD_DIM)
    always = pl.program_id(0) >= 0

    for hh in range(hps):
        for n in range(nb):
            kb = k_ref[n * blk:(n + 1) * blk, hcol(hh)]
            kmean_sc[hh, n:n + 1, :] = jnp.mean(kb, axis=0, keepdims=True)
            k16_sc[hh, n * blk:(n + 1) * blk, :] = kb.astype(BF16)
            vt_sc[hh, :, n * blk:(n + 1) * blk] = v_ref[n * blk:(n + 1) * blk, hcol(hh)].T.astype(BF16)

    for own in range(nb):
        @pl.when(always)
        def _(own=own):
            rows = slice(own * blk, (own + 1) * blk)
            key_i = lax.broadcasted_iota(jnp.int32, (blk, blk), 0)
            qry_i = lax.broadcasted_iota(jnp.int32, (blk, blk), 1)
            causal = key_i <= qry_i
            q16s, selbs = [], []
            for hh in range(hps):
                q = q_ref[rows, hcol(hh)]
                q16s.append(q.astype(BF16))
                if own == 0:
                    selbs.append(None)
                    continue
                s_blk = lax.dot_general(kmean_sc[hh], q, (((1,), (1,)), ((), ())),
                                        precision=lax.Precision.HIGHEST, preferred_element_type=F32)
                n_iota = lax.broadcasted_iota(jnp.int32, s_blk.shape, 0)
                cnt = jnp.zeros(s_blk.shape, F32)
                for m in range(own):
                    s_m = s_blk[m:m + 1, :]
                    cnt = cnt + jnp.where(s_m > s_blk, 1.0, jnp.where((s_m == s_blk) & (n_iota > m), 1.0, 0.0))
                sel = (n_iota < own) & (cnt < float(MOBA_TOPK))
                selbs.append(jnp.where(sel, 0.0, NEG))
            scores = []
            for hh in range(hps):
                sh = []
                for kb in range(own + 1):
                    s = lax.dot_general(k16_sc[hh, kb * blk:(kb + 1) * blk, :], q16s[hh], (((1,), (1,)), ((), ())),
                                        preferred_element_type=F32) * scale
                    sh.append(jnp.where(causal, s, NEG) if kb == own else s + selbs[hh][kb:kb + 1, :])
                scores.append(sh)
            probs, inv_l = [], []
            for hh in range(hps):
                m = functools.reduce(jnp.maximum, [jnp.max(s, axis=0, keepdims=True) for s in scores[hh]])
                ps = [jnp.exp(s - m) for s in scores[hh]]
                l = functools.reduce(jnp.add, [jnp.sum(p, axis=0, keepdims=True) for p in ps])
                inv_l.append(1.0 / l)
                probs.append([p.astype(BF16) for p in ps])
            for hh in range(hps):
                acc = jnp.zeros((HEAD_DIM, blk), F32)
                for kb, p16 in enumerate(probs[hh]):
                    acc = acc + jnp.dot(vt_sc[hh, :, kb * blk:(kb + 1) * blk], p16,
                                        preferred_element_type=F32)
                o_ref[rows, hcol(hh)] = (acc * inv_l[hh]).T


def _moba_prompt(q_arr, k_arr, v_arr, *, batch, seq, n_heads):
    nb = max(-(-seq // MOBA_BLOCK), MOBA_TOPK)
    assert seq % MOBA_BLOCK == 0 and nb * MOBA_BLOCK == seq
    hps = MOBA_HEADS_PER_STEP
    assert n_heads % hps == 0
    cw = hps * HEAD_DIM
    seq_heads = pl.BlockSpec((seq, cw), lambda b, h: (b, h))
    return pl.pallas_call(
        functools.partial(_moba_prompt_kernel, nb=nb),
        out_shape=jax.ShapeDtypeStruct((batch * seq, n_heads * HEAD_DIM), F32),
        grid=(batch, n_heads // hps),
        in_specs=[seq_heads, seq_heads, seq_heads],
        out_specs=seq_heads,
        scratch_shapes=[pltpu.VMEM((hps, nb, HEAD_DIM), F32),
                        pltpu.VMEM((hps, seq, HEAD_DIM), BF16),
                        pltpu.VMEM((hps, HEAD_DIM, seq), BF16)],
        compiler_params=_cparams(("parallel", "parallel")),
        name="moba_prompt",
    )(q_arr, k_arr, v_arr)


def _moba_sample_kernel(*refs, nblk, bps, ppb):
    npg = bps * ppb
    q_ref, kn_ref, vn_ref = refs[1:4]
    k_refs, v_refs = refs[4:4 + npg], refs[4 + npg:4 + 2 * npg]
    o_ref, ksum_sc, m_sc, l_sc, acc_sc = refs[4 + 2 * npg:]
    j = pl.program_id(1)
    scale = HEAD_DIM ** -0.5
    q = q_ref[...]
    q16 = q.astype(BF16)
    n_heads = q.shape[0]
    rows = k_refs[0].shape[0] * n_heads
    lane_head = lax.broadcasted_iota(jnp.int32, (n_heads, rows), 1) % n_heads
    mine = lane_head == lax.broadcasted_iota(jnp.int32, (n_heads, rows), 0)
    scores = []
    for b in range(bps):
        ksum = None
        for k_ref in k_refs[b * ppb:(b + 1) * ppb]:
            kp = k_ref[...]
            kpsum = jnp.sum(kp, axis=0)
            ksum = kpsum if ksum is None else ksum + kpsum
            s = lax.dot_general(q16, kp.reshape(rows, HEAD_DIM).astype(BF16), (((1,), (1,)), ((), ())),
                                preferred_element_type=F32) * scale
            scores.append(jnp.where(mine, s, NEG))
        ksum_sc[j * bps + b] = ksum
    probs = []
    for b in range(bps):
        sb = scores[b * ppb:(b + 1) * ppb]
        m = functools.reduce(jnp.maximum, [jnp.max(s, axis=1, keepdims=True) for s in sb])
        ps = [jnp.exp(s - m) for s in sb]
        l = functools.reduce(jnp.add, [jnp.sum(p, axis=1, keepdims=True) for p in ps])
        m_sc[j * bps + b] = jnp.broadcast_to(m, (n_heads, HEAD_DIM))
        l_sc[j * bps + b] = jnp.broadcast_to(l, (n_heads, HEAD_DIM))
        probs.append([p.astype(BF16) for p in ps])
    for b in range(bps):
        acc = jnp.zeros((n_heads, HEAD_DIM), F32)
        for p16, v_ref in zip(probs[b], v_refs[b * ppb:(b + 1) * ppb]):
            acc = acc + jnp.dot(p16, v_ref[...].reshape(rows, HEAD_DIM).astype(BF16),
                                preferred_element_type=F32)
        acc_sc[j * bps + b] = acc

    @pl.when(j == nblk // bps - 1)
    def _():
        kmean = ksum_sc[...] * (1.0 / MOBA_BLOCK)
        s_blk = jnp.sum(kmean * q[None], axis=-1, keepdims=True)
        n_iota = lax.broadcasted_iota(jnp.int32, s_blk.shape, 0)
        cnt = jnp.zeros(s_blk.shape, F32)
        for mm in range(nblk):
            s_m = s_blk[mm:mm + 1]
            cnt = cnt + jnp.where(s_m > s_blk, 1.0, jnp.where((s_m == s_blk) & (n_iota > mm), 1.0, 0.0))
        sel = cnt < float(MOBA_TOPK)
        m_b = m_sc[...]
        s_own = jnp.sum(q * kn_ref[...], axis=-1, keepdims=True) * scale
        m_tot = jnp.maximum(s_own, jnp.max(jnp.where(sel, m_b, NEG), axis=0))
        w_b = jnp.where(sel, jnp.exp(m_b - m_tot[None]), 0.0)
        w_own = jnp.exp(s_own - m_tot)
        num = w_own * vn_ref[...] + jnp.sum(w_b * acc_sc[...], axis=0)
        den = w_own + jnp.sum(w_b * l_sc[...], axis=0)
        o_ref[...] = num / den


def _moba_sample(q, k_new, v_new, cache_k, cache_v, page_table, layer):
    n, n_heads, _ = q.shape
    page = cache_k.shape[2]
    n_pages = page_table.shape[1]
    assert MOBA_BLOCK % page == 0 and (n_pages * page) % MOBA_BLOCK == 0
    ppb = MOBA_BLOCK // page
    nblk = n_pages * page // MOBA_BLOCK
    assert nblk >= MOBA_TOPK
    bps = next(b for b in (8, 4, 2, 1) if nblk % b == 0)
    npg = bps * ppb
    vec = pl.BlockSpec((None, n_heads, HEAD_DIM), lambda i, j, pt: (i, 0, 0))

    def page_spec(which):
        return pl.BlockSpec((None, None, page, n_heads, HEAD_DIM),
                            lambda i, j, pt: (layer, pt[i, npg * j + which], 0, 0, 0))

    pages = [page_spec(w) for w in range(npg)]
    grid_spec = pltpu.PrefetchScalarGridSpec(
        num_scalar_prefetch=1, grid=(n, nblk // bps),
        in_specs=[vec, vec, vec] + pages + pages,
        out_specs=pl.BlockSpec((None, n_heads, HEAD_DIM), lambda i, j, pt: (i, 0, 0)),
        scratch_shapes=[pltpu.VMEM((nblk, n_heads, HEAD_DIM), F32)] * 4)
    return pl.pallas_call(
        functools.partial(_moba_sample_kernel, nblk=nblk, bps=bps, ppb=ppb),
        out_shape=jax.ShapeDtypeStruct((n, n_heads, HEAD_DIM), F32),
        grid_spec=grid_spec,
        compiler_params=_cparams(("parallel", "arbitrary")),
        name="moba_sample",
    )(page_table, q, k_new, v_new, *([cache_k] * npg), *([cache_v] * npg))


def _s5_kernel(u_ref, x0_ref, bb_ref, cc_ref, ar_ref, ai_ref, d_ref, gw_ref, gb_ref,
               o_ref, st_ref, state_sc, xs_sc, y_sc, *, steps, nb, n_slabs):
    c = pl.program_id(0)
    half = S5_SLAB_GROUPS * S5_STATE
    wslab = S5_SLAB_GROUPS * S5_GROUP
    rows = steps * nb

    @pl.when(c == 0)
    def _():
        state_sc[...] = x0_ref[...]

    for gs in range(n_slabs):
        cols = slice(gs * wslab, (gs + 1) * wslab)
        ug = u_ref[:, :, cols].reshape(rows, wslab)
        xs_sc[...] = jnp.dot(ug.astype(BF16), bb_ref[gs], preferred_element_type=F32)
        ar = jnp.broadcast_to(ar_ref[gs], (nb, half))
        ai = jnp.broadcast_to(ai_ref[gs], (nb, half))

        def step(t, carry):
            xr, xi = carry
            r0 = pl.multiple_of(t * nb, nb)
            nxr = ar * xr - ai * xi + xs_sc[pl.ds(r0, nb), 0:half]
            nxi = ar * xi + ai * xr + xs_sc[pl.ds(r0, nb), half:2 * half]
            xs_sc[pl.ds(r0, nb), 0:half] = nxr
            xs_sc[pl.ds(r0, nb), half:2 * half] = nxi
            return nxr, nxi

        carry = (state_sc[gs, 0], state_sc[gs, 1])
        if steps == 1:
            xr, xi = step(0, carry)
        else:
            xr, xi = lax.fori_loop(0, steps, step, carry, unroll=4)
        state_sc[gs, 0] = xr
        state_sc[gs, 1] = xi
        y = jnp.dot(xs_sc[...].astype(BF16), cc_ref[gs], preferred_element_type=F32)
        y_sc[:, cols] = y + d_ref[:, cols] * ug

    z = jax.nn.gelu(y_sc[...])
    gate = jnp.dot(z.astype(BF16), gw_ref[...], preferred_element_type=F32) + gb_ref[...]
    o_ref[...] = (z * jax.nn.sigmoid(gate)).reshape(o_ref.shape)

    @pl.when(c == pl.num_programs(0) - 1)
    def _():
        st_ref[...] = state_sc[...]


def _s5(u_tb, x0, bb, cc, ar, ai, d, gw, gb, *, steps):
    t, nb, w = u_tb.shape
    n_slabs = bb.shape[0]
    half = S5_SLAB_GROUPS * S5_STATE
    rows = steps * nb
    const = lambda *shape: pl.BlockSpec(shape, lambda c: (0,) * len(shape))
    return pl.pallas_call(
        functools.partial(_s5_kernel, steps=steps, nb=nb, n_slabs=n_slabs),
        out_shape=(jax.ShapeDtypeStruct((t, nb, w), F32),
                   jax.ShapeDtypeStruct((n_slabs, 2, nb, half), F32)),
        grid=(t // steps,),
        in_specs=[pl.BlockSpec((steps, nb, w), lambda c: (c, 0, 0)),
                  const(n_slabs, 2, nb, half), const(*bb.shape), const(*cc.shape),
                  const(*ar.shape), const(*ai.shape), const(1, w), const(w, w), const(1, w)],
        out_specs=(pl.BlockSpec((steps, nb, w), lambda c: (c, 0, 0)), const(n_slabs, 2, nb, half)),
        scratch_shapes=[pltpu.VMEM((n_slabs, 2, nb, half), F32),
                        pltpu.VMEM((rows, 2 * half), F32),
                        pltpu.VMEM((rows, w), F32)],
        compiler_params=_cparams(("arbitrary",)),
        name="s5_scan",
    )(u_tb, x0, bb, cc, ar, ai, d, gw, gb)


def _s5_params(lam_re, lam_im, log_dt, b_ri, c_ri):
    g, p = lam_re.shape
    ns = g // S5_SLAB_GROUPS
    dt = jnp.exp(log_dt.astype(F32))[:, None]
    lr, li = lam_re.astype(F32), lam_im.astype(F32)
    mag = jnp.exp(lr * dt)
    ar, ai = mag * jnp.cos(li * dt), mag * jnp.sin(li * dt)
    d2 = lr * lr + li * li
    cr = ((ar - 1.0) * lr + ai * li) / d2
    ci = (ai * lr - (ar - 1.0) * li) / d2
    br, bi = b_ri[..., 0].astype(F32), b_ri[..., 1].astype(F32)
    bbr = cr[..., None] * br - ci[..., None] * bi
    bbi = cr[..., None] * bi + ci[..., None] * br
    eye = jnp.eye(S5_SLAB_GROUPS, dtype=F32)

    def blockdiag_in(m):
        m = m.reshape(ns, S5_SLAB_GROUPS, p, S5_GROUP)
        return jnp.einsum('sipc,ij->sicjp', m, eye).reshape(ns, S5_SLAB_GROUPS * S5_GROUP, S5_SLAB_GROUPS * p)

    def blockdiag_out(m):
        m = m.reshape(ns, S5_SLAB_GROUPS, S5_GROUP, p)
        return jnp.einsum('sicp,ij->sipjc', m, eye).reshape(ns, S5_SLAB_GROUPS * p, S5_SLAB_GROUPS * S5_GROUP)

    bb = jnp.concatenate([blockdiag_in(bbr), blockdiag_in(bbi)], axis=-1).astype(BF16)
    c_re, c_im = c_ri[..., 0].astype(F32), c_ri[..., 1].astype(F32)
    cc = jnp.concatenate([blockdiag_out(c_re), blockdiag_out(-c_im)], axis=1).astype(BF16)
    ar_s = ar.reshape(ns, 1, S5_SLAB_GROUPS * p)
    ai_s = ai.reshape(ns, 1, S5_SLAB_GROUPS * p)
    return bb, cc, ar_s, ai_s


def _s5_state_to_slabs(x0):
    n, g, p, _ = x0.shape
    ns = g // S5_SLAB_GROUPS
    return x0.reshape(n, ns, S5_SLAB_GROUPS * p, 2).transpose(1, 3, 0, 2)


def _s5_state_from_slabs(st, g):
    ns, _, n, hp = st.shape
    return st.transpose(2, 0, 3, 1).reshape(n, g, hp // S5_SLAB_GROUPS, 2)


def _outproj_even_kernel(att_ref, ga_ref, so_ref, gb_ref, w_ref, res_ref, o_ref):
    wa = att_ref.shape[1]
    a0 = (att_ref[...] * _silu(ga_ref[...])).astype(BF16)
    a1 = (so_ref[...] * _silu(gb_ref[...])).astype(BF16)
    o_ref[...] = (res_ref[...] + jnp.dot(a0, w_ref[:wa, :], preferred_element_type=F32)
                  + jnp.dot(a1, w_ref[wa:, :], preferred_element_type=F32))


def _outproj_even(att, gates, so2d, so_index, w, res, *, ga_col, gb_col, tm):
    n, wa = att.shape
    d = w.shape[1]
    gac, gbc = ga_col // wa, gb_col // wa
    so_map = (lambda i: (i, 0)) if so_index is None else so_index
    return pl.pallas_call(
        _outproj_even_kernel,
        out_shape=jax.ShapeDtypeStruct((n, d), F32),
        grid=(n // tm,),
        in_specs=[pl.BlockSpec((tm, wa), lambda i: (i, 0)),
                  pl.BlockSpec((tm, wa), lambda i: (i, gac)),
                  pl.BlockSpec((tm, wa), so_map),
                  pl.BlockSpec((tm, wa), lambda i: (i, gbc)),
                  pl.BlockSpec(w.shape, lambda i: (0, 0), pipeline_mode=pl.Buffered(1)),
                  pl.BlockSpec((tm, d), lambda i: (i, 0))],
        out_specs=pl.BlockSpec((tm, d), lambda i: (i, 0)),
        compiler_params=_cparams(("parallel",)),
        name="outproj_even",
    )(att, gates, so2d, gates, w, res)


def _matmul_res_kernel(a_ref, w_ref, res_ref, o_ref):
    o_ref[...] = res_ref[...] + jnp.dot(a_ref[...], w_ref[...], preferred_element_type=F32)


def _matmul_res(a, w, res, *, tm):
    n, k = a.shape
    d = w.shape[1]
    return pl.pallas_call(
        _matmul_res_kernel,
        out_shape=jax.ShapeDtypeStruct((n, d), F32),
        grid=(n // tm,),
        in_specs=[pl.BlockSpec((tm, k), lambda i: (i, 0)),
                  pl.BlockSpec((k, d), lambda i: (0, 0), pipeline_mode=pl.Buffered(1)),
                  pl.BlockSpec((tm, d), lambda i: (i, 0))],
        out_specs=pl.BlockSpec((tm, d), lambda i: (i, 0)),
        compiler_params=_cparams(("parallel",)),
        name="outproj_odd",
    )(a, w, res)


def _softplus(x):
    return jnp.maximum(x, 0.0) + jnp.log1p(jnp.exp(-jnp.abs(x)))


def _ssd_prompt_kernel(z_ref, x_ref, b_ref, c_ref, dt_ref, cw_ref, cb_ref, dtb_ref, alog_ref, de_ref,
                       nw_ref, e2_ref, cum_ref, y_ref, st_ref, xpad_sc, act_sc, ht_sc, y_sc, *, chunk, width):
    c = pl.program_id(1)
    L = chunk
    hist = 8
    gw = width // SSD_GROUPS
    hpg = gw // SSD_HEAD_DIM
    bw = SSD_GROUPS * SSD_STATE
    pw = 2 * SSD_HEAD_DIM
    assert pw == 128 and hpg % 2 == 0

    @pl.when(c == 0)
    def _():
        xpad_sc[0:hist, :] = jnp.zeros((hist, xpad_sc.shape[1]), F32)
        ht_sc[...] = jnp.zeros(ht_sc.shape, F32)

    xpad_sc[hist:hist + L, 0:width] = x_ref[...]
    xpad_sc[hist:hist + L, width:width + bw] = b_ref[...]
    xpad_sc[hist:hist + L, width + bw:width + 2 * bw] = c_ref[...]
    cs = 512
    for j in range(xpad_sc.shape[1] // cs):
        sl = slice(j * cs, (j + 1) * cs)
        ext = xpad_sc[:, sl]
        conv = cw_ref[0:1, sl] * ext
        for k in range(1, SSD_CONV):
            conv = cw_ref[k:k + 1, sl] * ext + pltpu.roll(conv, 1, 0)
        act_sc[:, sl] = _silu(conv[hist:, :] + cb_ref[:, sl])
    xpad_sc[0:hist, :] = xpad_sc[L:L + hist, :]

    dt = _softplus(dt_ref[...] + dtb_ref[...])
    a = -jnp.exp(alog_ref[...])
    adt = dt * a
    a_cs_t = _dot3(adt.T, cum_ref[...])
    a_cs = a_cs_t.T
    ea = jnp.exp(a_cs)
    ds = jnp.exp(a_cs[L - 1:L, :] - a_cs)
    row_i = lax.broadcasted_iota(jnp.int32, (L, L), 0)
    col_i = lax.broadcasted_iota(jnp.int32, (L, L), 1)
    tri = row_i >= col_i
    first_head = lax.broadcasted_iota(jnp.int32, (L, pw), 1) < SSD_HEAD_DIM

    def hi_mid(v):
        hi, mid, _ = _split3(v)
        return jnp.concatenate([hi, mid], axis=1)

    dt_p, ea_p, ds_p = hi_mid(dt), hi_mid(ea), hi_mid(ds)

    for g in range(SSD_GROUPS):
        gsl = slice(g * gw, (g + 1) * gw)
        e2_g = e2_ref[:, gsl]
        dt_g = jnp.dot(dt_p, e2_g, preferred_element_type=F32)
        ea_g = jnp.dot(ea_p, e2_g, preferred_element_type=F32)
        ds_g = jnp.dot(ds_p, e2_g, preferred_element_type=F32)
        xs_g = act_sc[:, gsl]
        xdt_g = xs_g * dt_g
        b_g = act_sc[:, width + g * SSD_STATE:width + (g + 1) * SSD_STATE].astype(BF16)
        c_g = act_sc[:, width + bw + g * SSD_STATE:width + bw + (g + 1) * SSD_STATE].astype(BF16)
        cb = lax.dot_general(c_g, b_g, (((1,), (1,)), ((), ())), preferred_element_type=F32)
        cb = jnp.where(tri, cb, 0.0)
        h_prev = ht_sc[g]
        y_off = jnp.dot(c_g, h_prev.astype(BF16), preferred_element_type=F32) * ea_g
        states_t = lax.dot_general(b_g, (xdt_g * ds_g).astype(BF16), (((0,), (0,)), ((), ())),
                                   preferred_element_type=F32)
        ht_sc[g] = h_prev * ea_g[L - 1:L, :] + states_t
        xdt16 = xdt_g.astype(BF16)
        for pr in range(hpg // 2):
            psl = slice(pr * pw, (pr + 1) * pw)
            decays = []
            for h in (g * hpg + 2 * pr, g * hpg + 2 * pr + 1):
                diff = a_cs[:, h:h + 1] - a_cs_t[h:h + 1, :]
                decays.append((cb * jnp.exp(jnp.minimum(diff, 0.0))).astype(BF16))
            xp = xdt16[:, psl]
            zero = jnp.zeros_like(xp)
            rhs = jnp.concatenate([jnp.where(first_head, xp, zero), jnp.where(first_head, zero, xp)], axis=0)
            yd = jnp.dot(jnp.concatenate(decays, axis=1), rhs, preferred_element_type=F32)
            y_sc[:, psl] = yd + y_off[:, psl] + de_ref[:, g * gw + pr * pw:g * gw + (pr + 1) * pw] * xs_g[:, psl]
        y = y_sc[...] * _silu(z_ref[:, gsl])
        ms = jnp.mean(y * y, axis=-1, keepdims=True)
        y_ref[:, gsl] = (y * lax.rsqrt(ms + EPS) * nw_ref[:, gsl]).astype(y_ref.dtype)

    @pl.when(c == pl.num_programs(1) - 1)
    def _():
        for g in range(SSD_GROUPS):
            st_ref[g * gw:(g + 1) * gw, :] = ht_sc[g].T


def _ssd_prompt(proj, dt_raw, cw, cb, dtb, alog, de, nw, e2_mat, tril, *, batch, seq, width):
    L = SSD_CHUNK
    assert seq % L == 0
    nc = seq // L
    bw = SSD_GROUPS * SSD_STATE
    conv_dim = width + 2 * bw
    xcol = width // width
    bcol = (2 * width) // bw
    ccol = (2 * width + bw) // bw
    const = lambda *shape: pl.BlockSpec(shape, lambda b, c: (0,) * len(shape))
    return pl.pallas_call(
        functools.partial(_ssd_prompt_kernel, chunk=L, width=width),
        out_shape=(jax.ShapeDtypeStruct((batch * seq, width), BF16),
                   jax.ShapeDtypeStruct((batch, width, SSD_STATE), F32)),
        grid=(batch, nc),
        in_specs=[pl.BlockSpec((L, width), lambda b, c: (b * nc + c, 0)),
                  pl.BlockSpec((L, width), lambda b, c: (b * nc + c, xcol)),
                  pl.BlockSpec((L, bw), lambda b, c: (b * nc + c, bcol)),
                  pl.BlockSpec((L, bw), lambda b, c: (b * nc + c, ccol)),
                  pl.BlockSpec((L, 128), lambda b, c: (b * nc + c, 0)),
                  const(SSD_CONV, conv_dim), const(1, conv_dim), const(1, 128), const(1, 128),
                  const(1, width), const(1, width), const(256, width), const(L, L)],
        out_specs=(pl.BlockSpec((L, width), lambda b, c: (b * nc + c, 0)),
                   pl.BlockSpec((None, width, SSD_STATE), lambda b, c: (b, 0, 0))),
        scratch_shapes=[pltpu.VMEM((L + 8, conv_dim), F32),
                        pltpu.VMEM((L, conv_dim), F32),
                        pltpu.VMEM((SSD_GROUPS, SSD_STATE, width // SSD_GROUPS), F32),
                        pltpu.VMEM((L, width // SSD_GROUPS), F32)],
        compiler_params=_cparams(("parallel", "arbitrary")),
        name="ssd_prompt",
    )(proj, proj, proj, proj, dt_raw, cw, cb, dtb, alog, de, nw, e2_mat, tril)


def _conv_step_kernel(x_ref, s0_ref, s1_ref, s2_ref, cw_ref, cb_ref, dt_ref, dtb_ref, alog_ref,
                      act_ref, dto_ref, dao_ref):
    conv = (cb_ref[...] + cw_ref[0:1, :] * s0_ref[...] + cw_ref[1:2, :] * s1_ref[...]
            + cw_ref[2:3, :] * s2_ref[...] + cw_ref[3:4, :] * x_ref[...])
    act_ref[...] = _silu(conv)

    @pl.when(pl.program_id(0) == 0)
    def _():
        dt = _softplus(dt_ref[...] + dtb_ref[...])
        dto_ref[...] = dt
        dao_ref[...] = jnp.exp(dt * -jnp.exp(alog_ref[...]))


def _conv_step(proj, dt_raw, conv_state2d, cw, cb, dtb, alog, *, width):
    n = proj.shape[0]
    conv_dim = cw.shape[1]
    cs = 512
    nj = conv_dim // cs
    x0 = width // cs
    return pl.pallas_call(
        _conv_step_kernel,
        out_shape=(jax.ShapeDtypeStruct((n, conv_dim), F32), jax.ShapeDtypeStruct((n, 128), F32),
                   jax.ShapeDtypeStruct((n, 128), F32)),
        grid=(nj,),
        in_specs=[pl.BlockSpec((n, cs), lambda j: (0, x0 + j)),
                  pl.BlockSpec((n, cs), lambda j: (0, j)),
                  pl.BlockSpec((n, cs), lambda j: (0, nj + j)),
                  pl.BlockSpec((n, cs), lambda j: (0, 2 * nj + j)),
                  pl.BlockSpec((SSD_CONV, cs), lambda j: (0, j)),
                  pl.BlockSpec((1, cs), lambda j: (0, j)),
                  pl.BlockSpec((n, 128), lambda j: (0, 0)),
                  pl.BlockSpec((1, 128), lambda j: (0, 0)),
                  pl.BlockSpec((1, 128), lambda j: (0, 0))],
        out_specs=(pl.BlockSpec((n, cs), lambda j: (0, j)),
                   pl.BlockSpec((n, 128), lambda j: (0, 0)),
                   pl.BlockSpec((n, 128), lambda j: (0, 0))),
        compiler_params=_cparams(("arbitrary",)),
        name="ssd_conv_step",
    )(proj, conv_state2d, conv_state2d, conv_state2d, cw, cb, dt_raw, dtb, alog)


def _ssd_step_kernel(dt_sm, da_sm, h0_ref, xt_ref, b_ref, c_ref, z_ref, x_ref, de_ref, nw_ref,
                     hn_ref, y_ref, *, width, per_step):
    gw = width // SSD_GROUPS
    hpg = gw // SSD_HEAD_DIM
    nrow = xt_ref.shape[1]
    row_i = lax.broadcasted_iota(jnp.int32, (nrow, SSD_STATE), 0)
    for s in range(per_step):
        n = pl.program_id(0) * per_step + s
        rowsel = row_i == n
        for g in range(SSD_GROUPS):
            ssl = slice(g * SSD_STATE, (g + 1) * SSD_STATE)
            gsl = slice(g * gw, (g + 1) * gw)
            rhs = jnp.where(rowsel, b_ref[s, :, ssl], 0.0).astype(BF16)
            outer = jnp.dot(xt_ref[gsl, :], rhs, preferred_element_type=F32)
            for r in range(hpg):
                h = g * hpg + r
                rsl = slice(g * gw + r * SSD_HEAD_DIM, g * gw + (r + 1) * SSD_HEAD_DIM)
                hn_ref[s, rsl, :] = (da_sm[n, h] * h0_ref[s, rsl, :]
                                     + dt_sm[n, h] * outer[r * SSD_HEAD_DIM:(r + 1) * SSD_HEAD_DIM, :])
            c8 = jnp.broadcast_to(c_ref[s, :, ssl], (8, SSD_STATE)).astype(BF16)
            yg = lax.dot_general(c8, hn_ref[s, gsl, :].astype(BF16), (((1,), (1,)), ((), ())),
                                 preferred_element_type=F32)[0:1, :]
            y = (yg + de_ref[:, gsl] * x_ref[s, :, gsl]) * _silu(z_ref[s, :, gsl])
            ms = jnp.mean(y * y, axis=-1, keepdims=True)
            y_ref[s, :, gsl] = y * lax.rsqrt(ms + EPS) * nw_ref[:, gsl]


def _ssd_step(dt, da, h0, xt, bm, cm, z, xs, de, nw, *, width):
    n = h0.shape[0]
    bw = SSD_GROUPS * SSD_STATE
    per_step = 2 if n % 2 == 0 else 1
    row = lambda w: pl.BlockSpec((per_step, 1, w), lambda i: (i, 0, 0))
    state = pl.BlockSpec((per_step, width, SSD_STATE), lambda i: (i, 0, 0))
    smem = pl.BlockSpec(memory_space=pltpu.SMEM)
    return pl.pallas_call(
        functools.partial(_ssd_step_kernel, width=width, per_step=per_step),
        out_shape=(jax.ShapeDtypeStruct(h0.shape, F32), jax.ShapeDtypeStruct((n, 1, width), F32)),
        grid=(n // per_step,),
        in_specs=[smem, smem, state,
                  pl.BlockSpec((width, n), lambda i: (0, 0)),
                  row(bw), row(bw), row(width), row(width),
                  pl.BlockSpec((1, width), lambda i: (0, 0)),
                  pl.BlockSpec((1, width), lambda i: (0, 0))],
        out_specs=(state, row(width)),
        compiler_params=_cparams(("arbitrary",)),
        name="ssd_step",
    )(dt, da, h0, xt, bm, cm, z, xs, de, nw)


def _rope_tables(pos):
    half = HEAD_DIM // 2
    inv = ROPE_THETA ** (-jnp.arange(half, dtype=F32) / half)
    ang = pos.astype(F32)[:, None] * inv[None, :]
    cos, sin = jnp.cos(ang), jnp.sin(ang)
    return jnp.concatenate([cos, cos], axis=-1), jnp.concatenate([-sin, sin], axis=-1)


def _even_layer(yp, ys, e, norm_w_i, cache_k, cache_v, page_table, state_s5, w_in, qn, kn,
                lam_re, lam_im, log_dt, s5_b, s5_c, s5_d, glu_w, glu_b, w_out):
    batch, seq, d = yp.shape
    nsamp = ys.shape[0]
    assert ys.shape[1] == 1
    aw = d // 2
    n_heads = aw // HEAD_DIM
    n_past = page_table.shape[1] * cache_k.shape[2]
    groups = lam_re.shape[0]

    w_in16 = w_in.astype(BF16)
    w_out16 = w_out.astype(BF16)
    nw = norm_w_i.reshape(1, d)
    qn2, kn2 = qn.reshape(1, HEAD_DIM), kn.reshape(1, HEAD_DIM)
    cos_p, sin_p = _rope_tables(jnp.arange(seq, dtype=jnp.int32))
    cos_s, sin_s = _rope_tables(jnp.full((nsamp,), n_past, dtype=jnp.int32))

    xp2 = yp.reshape(batch * seq, d)
    xs2 = ys.reshape(nsamp, d)
    tn = 1024
    nt = aw // tn
    routes = [(0, 0, nt, "q"), (1, nt, 2 * nt, "k"), (2, 2 * nt, 3 * nt, "plain"),
              (0, 3 * nt, 4 * nt, "plain"), (3, 4 * nt, 5 * nt, "plain"), (0, 5 * nt, 6 * nt, "plain")]
    plain_map = lambda i, c: (i, c)
    tm_p = min(1024, seq)
    nper = seq // tm_p
    n_p = batch * seq
    outs_p = [((n_p, 3 * aw), plain_map), ((n_p, aw), plain_map), ((n_p, aw), plain_map),
              ((seq, batch * aw), lambda i, c: (i % nper, (i // nper) * nt + c))]
    outs_s = [((nsamp, 3 * aw), plain_map), ((nsamp, aw), plain_map), ((nsamp, aw), plain_map),
              ((nsamp, aw), plain_map)]
    qg_p, k_p, v_p, u_p = _norm_inproj(xp2, nw, w_in16, tm=tm_p, tn=tn, routes=routes, outs=outs_p,
                                       rope=(cos_p, sin_p, qn2, kn2), nper=nper)
    qg_s, k_s, v_s, u_s = _norm_inproj(xs2, nw, w_in16, tm=nsamp, tn=tn, routes=routes, outs=outs_s,
                                       rope=(cos_s, sin_s, qn2, kn2))

    att_p = _moba_prompt(qg_p, k_p, v_p, batch=batch, seq=seq, n_heads=n_heads)
    heads3 = lambda a: a.reshape(nsamp, n_heads, HEAD_DIM)
    att_s = _moba_sample(heads3(qg_s[:, :aw]), heads3(k_s), heads3(v_s), cache_k, cache_v, page_table,
                         e).reshape(nsamp, aw)

    bb, cc, ar, ai = _s5_params(lam_re, lam_im, log_dt, s5_b, s5_c)
    d2 = s5_d.reshape(1, aw).astype(F32)
    gw16 = glu_w.astype(BF16)
    gb2 = glu_b.reshape(1, aw).astype(F32)
    zero_state = jnp.zeros((groups // S5_SLAB_GROUPS, 2, batch, S5_SLAB_GROUPS * S5_STATE), F32)
    so_p, st_p = _s5(u_p.reshape(seq, batch, aw), zero_state, bb, cc, ar, ai, d2, gw16, gb2, steps=64)
    so_s, st_s = _s5(u_s.reshape(1, nsamp, aw), _s5_state_to_slabs(state_s5), bb, cc, ar, ai, d2, gw16, gb2,
                     steps=1)

    tm_o = 512
    nto = seq // tm_o
    yp_new = _outproj_even(att_p, qg_p, so_p.reshape(seq, batch * aw), lambda i: (i % nto, i // nto),
                           w_out16, xp2, ga_col=aw, gb_col=2 * aw, tm=tm_o)
    ys_new = _outproj_even(att_s, qg_s, so_s.reshape(nsamp, aw), None, w_out16, xs2,
                           ga_col=aw, gb_col=2 * aw, tm=nsamp)

    heads4 = lambda a, b, t: a.reshape(b, t, n_heads, HEAD_DIM)
    outs = (heads4(k_p, batch, seq), heads4(v_p, batch, seq), heads4(k_s, nsamp, 1), heads4(v_s, nsamp, 1),
            _s5_state_from_slabs(st_p, groups), _s5_state_from_slabs(st_s, groups))
    return yp_new.reshape(batch, seq, d), ys_new.reshape(nsamp, 1, d), outs


def _odd_layer(yp, ys, norm_w_i, state_conv, state_ssd, w_in, conv_w, conv_b, dt_bias, a_log, d_skip,
               ssd_norm_w, w_out):
    batch, seq, d = yp.shape
    nsamp = ys.shape[0]
    n_heads = a_log.shape[0]
    width = n_heads * SSD_HEAD_DIM
    bw = SSD_GROUPS * SSD_STATE
    conv_dim = width + 2 * bw
    odd_in = w_in.shape[1]
    assert odd_in == width + conv_dim + n_heads and n_heads <= 128
    nmain = width + conv_dim
    tn = _largest_tile(nmain, 1536, 256)

    w_main16 = jnp.pad(w_in.astype(BF16), ((0, 0), (0, -odd_in % 512)))
    w_dt16 = w_main16[:, nmain:nmain + 128]
    w_out16 = w_out.astype(BF16)
    nw = norm_w_i.reshape(1, d)
    pad128 = lambda v: jnp.pad(v.astype(F32), (0, 128 - n_heads)).reshape(1, 128)
    dtb, alog = pad128(dt_bias), pad128(a_log)
    de = jnp.repeat(d_skip.astype(F32), SSD_HEAD_DIM).reshape(1, width)
    gnw = ssd_norm_w.reshape(1, width).astype(F32)
    cb = conv_b.reshape(1, conv_dim).astype(F32)
    cw = conv_w.astype(F32)
    e_mat = (jnp.arange(128)[:, None] == (jnp.arange(width)[None, :] // SSD_HEAD_DIM)).astype(BF16)
    tril = (jnp.arange(SSD_CHUNK)[:, None] <= jnp.arange(SSD_CHUNK)[None, :]).astype(BF16)

    xp2 = yp.reshape(batch * seq, d)
    xs2 = ys.reshape(nsamp, d)
    plain_map = lambda i, c: (i, c)
    routes = [(0, 0, nmain // tn, "plain")]
    proj_p, dtr_p = _norm_inproj(xp2, nw, w_main16, tm=min(1024, seq), tn=tn, routes=routes,
                                 outs=[((batch * seq, nmain), plain_map)], side_w=w_dt16)
    proj_s, dtr_s = _norm_inproj(xs2, nw, w_main16, tm=nsamp, tn=tn, routes=routes,
                                 outs=[((nsamp, nmain), plain_map)], side_w=w_dt16)

    e2_mat = jnp.concatenate([e_mat, e_mat], axis=0)
    yn_p, st_p = _ssd_prompt(proj_p, dtr_p, cw, cb, dtb, alog, de, gnw, e2_mat, tril, batch=batch, seq=seq,
                             width=width)
    yp_new = _matmul_res(yn_p, w_out16, xp2, tm=512)

    act_s, dt_s, da_s = _conv_step(proj_s, dtr_s, state_conv.reshape(nsamp, (SSD_CONV - 1) * conv_dim), cw, cb,
                                   dtb, alog, width=width)
    xs_s = act_s[:, :width]
    hn_s, yn_s = _ssd_step(dt_s, da_s, state_ssd.reshape(nsamp, width, SSD_STATE), xs_s.T.astype(BF16),
                           act_s[:, width:width + bw].reshape(nsamp, 1, bw),
                           act_s[:, width + bw:].reshape(nsamp, 1, bw),
                           proj_s[:, :width].reshape(nsamp, 1, width), xs_s.reshape(nsamp, 1, width),
                           de, gnw, width=width)
    ys_new = _matmul_res(yn_s.reshape(nsamp, width).astype(BF16), w_out16, xs2, tm=nsamp)

    buf_p = proj_p.reshape(batch, seq, nmain)[:, seq - (SSD_CONV - 1):, width:width + conv_dim]
    buf_s = jnp.concatenate([state_conv[:, 1:, :], proj_s[:, width:width + conv_dim].reshape(nsamp, 1, conv_dim)], axis=1)
    outs = (buf_p, buf_s, st_p.reshape(batch, n_heads, SSD_HEAD_DIM, SSD_STATE),
            hn_s.reshape(nsamp, n_heads, SSD_HEAD_DIM, SSD_STATE))
    return yp_new.reshape(batch, seq, d), ys_new.reshape(nsamp, 1, d), outs


def kernel(x_prompt, x_sample, cache_k, cache_v, page_table, state_s5, state_conv, state_ssd, norm_w, w_in_even, q_norm_w, k_norm_w, s5_lambda_re, s5_lambda_im, s5_log_dt, s5_b, s5_c, s5_d, s5_glu_w, s5_glu_b, w_out_even, w_in_odd, conv_w, conv_b, ssd_dt_bias, ssd_a_log, ssd_d, ssd_norm_w, w_out_odd):
    depth = norm_w.shape[0]
    yp, ys = x_prompt, x_sample
    even_outs, odd_outs = [], []
    for i in range(depth):
        if i % 2 == 0:
            e = i // 2
            yp, ys, outs = _even_layer(yp, ys, e, norm_w[i], cache_k, cache_v, page_table, state_s5[e],
                                       w_in_even[e], q_norm_w[e], k_norm_w[e], s5_lambda_re[e], s5_lambda_im[e],
                                       s5_log_dt[e], s5_b[e], s5_c[e], s5_d[e], s5_glu_w[e], s5_glu_b[e], w_out_even[e])
            even_outs.append(outs)
        else:
            o = i // 2
            yp, ys, outs = _odd_layer(yp, ys, norm_w[i], state_conv[o], state_ssd[o], w_in_odd[o], conv_w[o],
                                      conv_b[o], ssd_dt_bias[o], ssd_a_log[o], ssd_d[o], ssd_norm_w[o], w_out_odd[o])
            odd_outs.append(outs)
    stack = lambda outs, k: jnp.stack([o[k] for o in outs])
    return (yp, ys, stack(even_outs, 0), stack(even_outs, 1), stack(even_outs, 2), stack(even_outs, 3),
            stack(even_outs, 4), stack(even_outs, 5), stack(odd_outs, 0), stack(odd_outs, 1),
            stack(odd_outs, 2), stack(odd_outs, 3))
```
